```python
import jax, jax.numpy as jnp
from jax import lax
import numpy as np

D_MODEL = 1024
BATCH = 2
SEQ = 16384
DEPTH = 2

N_A = DEPTH // 2
N_B = DEPTH - N_A
PLE_DIM = 256
EPS = 1e-6
NEG = -1e30
BIG = 1e30

RET_HEADS = D_MODEL // 256
RET_QK_DIM = D_MODEL // RET_HEADS
RET_V_DIM = 2 * RET_QK_DIM
RET_QK_W = RET_HEADS * RET_QK_DIM
RET_V_W = RET_HEADS * RET_V_DIM
RET_CHUNK = 128
ROPE_BASE = 10000.0

NSA_DK = 128
NSA_DV = 128
NSA_W = 2 * D_MODEL
NSA_HEADS = NSA_W // NSA_DV
NSA_GROUPS = 4
NSA_HPG = NSA_HEADS // NSA_GROUPS
NSA_KV_W = NSA_GROUPS * NSA_DK
CMP_LEN = 32
CMP_STRIDE = 16
CMP_HIDDEN = 256
SEL_BLOCK = 64
N_SEL = 16
WIN = 512
Q_BLOCK = 128

kernel_name = "yoco_retnet_nsa_hybrid"


def rmsnorm(x, g):
    xf = x.astype(jnp.float32)
    y = xf * lax.rsqrt(jnp.mean(xf * xf, axis=-1, keepdims=True) + EPS)
    return (y * g.astype(jnp.float32)).astype(x.dtype)


def rotary(x, pos):
    half = x.shape[-1] // 2
    inv = ROPE_BASE ** (-jnp.arange(half, dtype=jnp.float32) / half)
    ang = pos.astype(jnp.float32)[:, None] * inv[None, :]
    cos, sin = jnp.cos(ang), jnp.sin(ang)
    x1, x2 = x[..., :half], x[..., half:]
    return jnp.concatenate([x1 * cos - x2 * sin, x1 * sin + x2 * cos], axis=-1)


def retention_chunkwise(q, k, v):
    b, h, t, _ = q.shape
    nc = t // RET_CHUNK
    lg = jnp.log1p(-(2.0 ** (-5.0 - jnp.arange(h, dtype=jnp.float32))))
    pos = jnp.arange(RET_CHUNK, dtype=jnp.float32)
    diff = pos[:, None] - pos[None, :]
    decay = jnp.where(diff[None] >= 0,
                      jnp.exp(jnp.maximum(diff, 0.0)[None] * lg[:, None, None]), 0.0)
    q_dec = jnp.exp((pos + 1.0)[None, :] * lg[:, None])
    k_dec = jnp.exp((RET_CHUNK - 1.0 - pos)[None, :] * lg[:, None])
    chunk_dec = jnp.exp(RET_CHUNK * lg)

    def to_chunks(a):
        return jnp.moveaxis(a.reshape(b, h, nc, RET_CHUNK, a.shape[-1]), 2, 0)

    def step(state, qkv):
        qc, kc, vc = qkv
        s = jnp.einsum('bhnd,bhmd->bhnm', qc, kc) * decay
        o = (jnp.einsum('bhnm,bhmv->bhnv', s, vc)
             + jnp.einsum('bhnd,bhdv->bhnv', qc, state) * q_dec[..., None])
        state = (state * chunk_dec[:, None, None]
                 + jnp.einsum('bhmd,bhmv->bhdv', kc * k_dec[..., None], vc))
        return state, o

    state0 = jnp.zeros((b, h, q.shape[-1], v.shape[-1]), jnp.float32)
    _, o = lax.scan(step, state0, (to_chunks(q), to_chunks(k), to_chunks(v)))
    return jnp.moveaxis(o, 0, 2).reshape(b, h, t, v.shape[-1])


def retention_layer(h, g_norm, w_in, gn_gain, w_out):
    b, t, _ = h.shape
    proj = rmsnorm(h, g_norm) @ w_in
    q, k, v, gate = jnp.split(proj, [RET_QK_W, 2 * RET_QK_W, 2 * RET_QK_W + RET_V_W], axis=-1)

    def heads(a, d):
        return a.reshape(b, t, RET_HEADS, d).transpose(0, 2, 1, 3).astype(jnp.float32)

    pos = jnp.arange(t)
    q = rotary(heads(q, RET_QK_DIM), pos)
    k = rotary(heads(k, RET_QK_DIM), pos) * (RET_QK_DIM ** -0.5)
    o = retention_chunkwise(q, k, heads(v, RET_V_DIM))
    mu = jnp.mean(o, axis=-1, keepdims=True)
    var = jnp.mean(jnp.square(o - mu), axis=-1, keepdims=True)
    o = ((o - mu) * lax.rsqrt(var + EPS)).transpose(0, 2, 1, 3).reshape(b, t, RET_V_W)
    o = o * gn_gain.astype(jnp.float32)
    y = (jax.nn.silu(gate.astype(jnp.float32)) * o).astype(h.dtype) @ w_out
    return h + y


def per_layer_embedding(h, p_i, g, w_gate, w_emb):
    gate = jax.nn.sigmoid(rmsnorm(h, g) @ w_gate)
    return h + gate * (p_i @ w_emb)


def compress_blocks(a, pe, w1, w2):
    b, g, t, d = a.shape
    lc = CMP_LEN // CMP_STRIDE
    n_cmp = (t - CMP_LEN) // CMP_STRIDE + 1
    sub = a.reshape(b, g, t // CMP_STRIDE, CMP_STRIDE, d)
    blocks = jnp.concatenate([sub[:, :, o:o + n_cmp] for o in range(lc)], axis=3)
    blocks = (blocks + pe).reshape(b, g, n_cmp, CMP_LEN * d)
    return jax.nn.gelu(blocks @ w1) @ w2


def nsa_shared_kv(h, g_kv, w_kv, pe_k, w1_k, w2_k, pe_v, w1_v, w2_v):
    b, t, _ = h.shape
    proj = rmsnorm(h, g_kv) @ w_kv
    parts = proj.reshape(b, t, 6, NSA_GROUPS, NSA_DK).transpose(2, 0, 3, 1, 4)
    kc, vc, ks, vs, kw, vw = parts[0], parts[1], parts[2], parts[3], parts[4], parts[5]
    k_cmp = compress_blocks(kc, pe_k, w1_k, w2_k)
    v_cmp = compress_blocks(vc, pe_v, w1_v, w2_v)
    nslc = t // SEL_BLOCK
    k_sel = ks.reshape(b, NSA_GROUPS, nslc, SEL_BLOCK * NSA_DK)
    v_sel = vs.reshape(b, NSA_GROUPS, nslc, SEL_BLOCK * NSA_DV)
    pad = ((0, 0), (0, 0), (WIN, 0), (0, 0))
    return (k_cmp, v_cmp, k_sel, v_sel, jnp.pad(kw, pad), jnp.pad(vw, pad))


def selection_importance(p, nslc):
    f = SEL_BLOCK // CMP_STRIDE
    lc = CMP_LEN // CMP_STRIDE
    left = lc - 1
    right = f * nslc - p.shape[-1]
    pp = jnp.pad(p, ((0, 0), (0, 0), (0, 0), (left, right)))
    terms = []
    for o in range(-(lc - 1), f):
        w = float(min(o + lc, f) - max(o, 0))
        s = left + o
        terms.append(w * pp[..., s:s + f * (nslc - 1) + 1:f])
    return sum(terms[1:], terms[0])


def nsa_query_block(q, gates, blk, k_cmp, v_cmp, k_sel, v_sel, k_win, v_win):
    b = q.shape[0]
    scale = NSA_DK ** -0.5
    t = blk * Q_BLOCK + jnp.arange(Q_BLOCK)

    n_cmp = k_cmp.shape[2]
    cmp_end = jnp.arange(n_cmp) * CMP_STRIDE + CMP_LEN - 1
    valid_c = cmp_end[None, :] <= t[:, None]
    s_c = jnp.einsum('bgrqd,bgnd->bgrqn', q, k_cmp).astype(jnp.float32) * scale
    p_c = jnp.where(valid_c, jax.nn.softmax(jnp.where(valid_c, s_c, NEG), axis=-1), 0.0)
    o_c = jnp.einsum('bgrqn,bgnd->bgrqd', p_c, v_cmp.astype(jnp.float32))

    nslc = k_sel.shape[2]
    n_top = min(N_SEL, nslc)
    imp = selection_importance(jnp.sum(p_c, axis=2), nslc)
    cur = (t // SEL_BLOCK)[:, None]
    j = jnp.arange(nslc)[None, :]
    forced = (j == 0) | (j == cur) | (j == cur - 1)
    score = jnp.where(j <= cur, jnp.where(forced, BIG, imp), -BIG)
    top_v, top_i = lax.top_k(score, n_top)
    sel_ok = top_v > -0.5 * BIG
    idx = top_i.reshape(b, NSA_GROUPS, Q_BLOCK * n_top)
    bi = jnp.arange(b)[:, None, None]
    gi = jnp.arange(NSA_GROUPS)[None, :, None]
    ks = k_sel[bi, gi, idx].reshape(b, NSA_GROUPS, Q_BLOCK, n_top, SEL_BLOCK, NSA_DK)
    vs = v_sel[bi, gi, idx].reshape(b, NSA_GROUPS, Q_BLOCK, n_top, SEL_BLOCK, NSA_DV)
    kpos = top_i[..., None] * SEL_BLOCK + jnp.arange(SEL_BLOCK)
    valid_s = (sel_ok[..., None] & (kpos <= t[:, None, None]))[:, :, None]
    s_s = jnp.einsum('bgrqd,bgqnkd->bgrqnk', q, ks).astype(jnp.float32) * scale
    p_s = jnp.where(valid_s, jax.nn.softmax(jnp.where(valid_s, s_s, NEG), axis=(-2, -1)), 0.0)
    o_s = jnp.einsum('bgrqnk,bgqnkd->bgrqd', p_s, vs.astype(jnp.float32))

    start = blk * Q_BLOCK
    kw = lax.dynamic_slice_in_dim(k_win, start, WIN + Q_BLOCK, axis=2)
    vw = lax.dynamic_slice_in_dim(v_win, start, WIN + Q_BLOCK, axis=2)
    kwpos = start - WIN + jnp.arange(WIN + Q_BLOCK)
    valid_w = ((kwpos[None, :] <= t[:, None]) & (kwpos[None, :] > t[:, None] - WIN)
               & (kwpos[None, :] >= 0))
    s_w = jnp.einsum('bgrqd,bgkd->bgrqk', q, kw).astype(jnp.float32) * scale
    p_w = jax.nn.softmax(jnp.where(valid_w, s_w, NEG), axis=-1)
    o_w = jnp.einsum('bgrqk,bgkd->bgrqd', p_w, vw.astype(jnp.float32))

    return gates[..., 0:1] * o_c + gates[..., 1:2] * o_s + gates[..., 2:3] * o_w


def nsa_layer(h, g_norm, w_in, w_out, k_cmp, v_cmp, k_sel, v_sel, k_win, v_win):
    b, t, _ = h.shape
    proj = rmsnorm(h, g_norm) @ w_in
    q_w = NSA_HEADS * NSA_DK
    q, gate, bgate = jnp.split(proj, [q_w, q_w + NSA_W], axis=-1)
    nqb = t // Q_BLOCK
    qb = q.reshape(b, nqb, Q_BLOCK, NSA_GROUPS, NSA_HPG, NSA_DK).transpose(1, 0, 3, 4, 2, 5)
    gb = jax.nn.sigmoid(bgate.astype(jnp.float32)).reshape(
        b, nqb, Q_BLOCK, NSA_GROUPS, NSA_HPG, 3).transpose(1, 0, 3, 4, 2, 5)

    def body(args):
        q_blk, g_blk, blk = args
        return nsa_query_block(q_blk, g_blk, blk, k_cmp, v_cmp, k_sel, v_sel, k_win, v_win)

    o = lax.map(body, (qb, gb, jnp.arange(nqb)))
    o = o.transpose(1, 0, 4, 2, 3, 5).reshape(b, t, NSA_W)
    y = (jax.nn.silu(gate.astype(jnp.float32)) * o).astype(h.dtype) @ w_out
    return h + y


def setup_inputs(seed: int = 0) -> dict:
    key = jax.random.key(seed)
    ks = jax.random.split(key, 24)

    def nrm(k, shape, scale):
        return jax.random.normal(k, shape, jnp.float32) * scale

    def gain(k, shape):
        return 1.0 + 0.02 * jax.random.normal(k, shape, jnp.float32)

    ret_in_w = 2 * RET_QK_W + 2 * RET_V_W
    nsa_in_w = NSA_HEADS * NSA_DK + NSA_W + 3 * NSA_HEADS
    return {
        "x": nrm(ks[0], (BATCH, SEQ, D_MODEL), 1.0),
        "p": nrm(ks[1], (DEPTH, BATCH, SEQ, PLE_DIM), 1.0),
        "ret_norm": gain(ks[2], (N_A, D_MODEL)),
        "ret_w_in": nrm(ks[3], (N_A, D_MODEL, ret_in_w), D_MODEL ** -0.5),
        "ret_gn": gain(ks[4], (N_A, RET_V_W)),
        "ret_w_out": nrm(ks[5], (N_A, RET_V_W, D_MODEL), RET_V_W ** -0.5),
        "kv_norm": gain(ks[6], (D_MODEL,)),
        "kv_w": nrm(ks[7], (D_MODEL, 6 * NSA_KV_W), D_MODEL ** -0.5),
        "cmp_pe_k": nrm(ks[8], (CMP_LEN, NSA_DK), 0.02),
        "cmp_w1_k": nrm(ks[9], (CMP_LEN * NSA_DK, CMP_HIDDEN), (CMP_LEN * NSA_DK) ** -0.5),
        "cmp_w2_k": nrm(ks[10], (CMP_HIDDEN, NSA_DK), CMP_HIDDEN ** -0.5),
        "cmp_pe_v": nrm(ks[11], (CMP_LEN, NSA_DV), 0.02),
        "cmp_w1_v": nrm(ks[12], (CMP_LEN * NSA_DV, CMP_HIDDEN), (CMP_LEN * NSA_DV) ** -0.5),
        "cmp_w2_v": nrm(ks[13], (CMP_HIDDEN, NSA_DV), CMP_HIDDEN ** -0.5),
        "nsa_norm": gain(ks[14], (N_B, D_MODEL)),
        "nsa_w_in": nrm(ks[15], (N_B, D_MODEL, nsa_in_w), D_MODEL ** -0.5),
        "nsa_w_out": nrm(ks[16], (N_B, NSA_W, D_MODEL), NSA_W ** -0.5),
        "ple_norm": gain(ks[17], (DEPTH, D_MODEL)),
        "ple_w_gate": nrm(ks[18], (DEPTH, D_MODEL, D_MODEL), D_MODEL ** -0.5),
        "ple_w_emb": nrm(ks[19], (DEPTH, PLE_DIM, D_MODEL), PLE_DIM ** -0.5),
        "final_norm": gain(ks[20], (D_MODEL,)),
    }


def reference(x, p, ret_norm, ret_w_in, ret_gn, ret_w_out, kv_norm, kv_w,
              cmp_pe_k, cmp_w1_k, cmp_w2_k, cmp_pe_v, cmp_w1_v, cmp_w2_v,
              nsa_norm, nsa_w_in, nsa_w_out, ple_norm, ple_w_gate, ple_w_emb, final_norm):
    h = x
    shared = None
    for i in range(DEPTH):
        if i < N_A:
            h = retention_layer(h, ret_norm[i], ret_w_in[i], ret_gn[i], ret_w_out[i])
        else:
            if i == N_A:
                shared = nsa_shared_kv(h, kv_norm, kv_w, cmp_pe_k, cmp_w1_k, cmp_w2_k,
                                       cmp_pe_v, cmp_w1_v, cmp_w2_v)
            j = i - N_A
            h = nsa_layer(h, nsa_norm[j], nsa_w_in[j], nsa_w_out[j], *shared)
        h = per_layer_embedding(h, p[i], ple_norm[i], ple_w_gate[i], ple_w_emb[i])
    return rmsnorm(h, final_norm)
```

```python
import functools

import numpy as np
import jax
import jax.numpy as jnp
from jax import lax
from jax.experimental import pallas as pl
from jax.experimental.pallas import tpu as pltpu

F32 = jnp.float32
BF16 = jnp.bfloat16

EPS = 1e-6
NEG = -1e30
BIG = 1e30
REMOVED = -3e38

RET_HEADS = 4
RET_CHUNK = 128
ROPE_BASE = 10000.0

NSA_DK = 128
NSA_GROUPS = 4
NSA_HPG = 4
NSA_HEADS = NSA_GROUPS * NSA_HPG
CMP_LEN = 32
CMP_STRIDE = 16
SEL_BLOCK = 64
N_SEL = 16
WIN = 512
Q_BLOCK = 128
GATE_ROWS = 16

LANE = 128
ROW_TILE = 512
KEY_TILE = 512
RET_ROWS = 512
VMEM_LIMIT = 56 * 1024 * 1024


def _params(sem):
    return pltpu.CompilerParams(dimension_semantics=sem, vmem_limit_bytes=VMEM_LIMIT)


def _const_spec(shape):
    nd = len(shape)
    return pl.BlockSpec(shape, lambda *_: (0,) * nd)


def _rms_unit(x):
    return x * lax.rsqrt(jnp.mean(x * x, axis=-1, keepdims=True) + EPS)


def _nt_dot(a, b):
    return lax.dot_general(a, b, (((1,), (1,)), ((), ())), preferred_element_type=F32)


def _dot(a, b):
    return jnp.dot(a, b, preferred_element_type=F32)


def _ret_inproj_kernel(x_ref, g_ref, w_ref, cos_ref, sin_ref, o_ref, *, qk_w, v_w):
    xn = (_rms_unit(x_ref[...]) * g_ref[...]).astype(BF16)
    cos = cos_ref[...]
    sin = sin_ref[...]
    half = cos.shape[-1]
    dk = 2 * half
    k_scale = float(dk) ** -0.5
    for hd in range(2 * qk_w // dk):
        c0 = hd * dk
        acc = _dot(xn, w_ref[:, c0:c0 + dk])
        x1 = acc[:, :half]
        x2 = acc[:, half:]
        r1 = x1 * cos - x2 * sin
        r2 = x1 * sin + x2 * cos
        if c0 >= qk_w:
            r1 = r1 * k_scale
            r2 = r2 * k_scale
        o_ref[:, c0:c0 + half] = r1.astype(BF16)
        o_ref[:, c0 + half:c0 + dk] = r2.astype(BF16)
    step = 512
    for c0 in range(2 * qk_w, 2 * qk_w + 2 * v_w, step):
        acc = _dot(xn, w_ref[:, c0:c0 + step])
        if c0 >= 2 * qk_w + v_w:
            acc = acc * jax.nn.sigmoid(acc)
        o_ref[:, c0:c0 + step] = acc.astype(BF16)


def _ret_inproj(h, g, w, cos, sin, seq):
    n, d = h.shape
    width = w.shape[1]
    qk_w = d
    v_w = (width - 2 * qk_w) // 2
    tm = ROW_TILE
    half = cos.shape[-1]
    pos_tiles = seq // tm
    return pl.pallas_call(
        functools.partial(_ret_inproj_kernel, qk_w=qk_w, v_w=v_w),
        grid=(n // tm,),
        in_specs=[
            pl.BlockSpec((tm, d), lambda i: (i, 0)),
            _const_spec((1, d)),
            _const_spec((d, width)),
            pl.BlockSpec((tm, half), lambda i: (i % pos_tiles, 0)),
            pl.BlockSpec((tm, half), lambda i: (i % pos_tiles, 0)),
        ],
        out_specs=pl.BlockSpec((tm, width), lambda i: (i, 0)),
        out_shape=jax.ShapeDtypeStruct((n, width), BF16),
        compiler_params=_params(("parallel",)),
        name="ret_inproj",
    )(h, g.reshape(1, d), w.astype(BF16), cos, sin)


def _retention_kernel(lg_ref, q_ref, k_ref, v_ref, sg_ref, gn_ref, o_ref, state_ref, *, chunk):
    @pl.when(pl.program_id(2) == 0)
    def _():
        state_ref[...] = jnp.zeros_like(state_ref)

    lg = lg_ref[0, 0:1, 0:1]
    row = lax.broadcasted_iota(jnp.int32, (chunk, chunk), 0)
    col = lax.broadcasted_iota(jnp.int32, (chunk, chunk), 1)
    diff = (row - col).astype(F32)
    decay = jnp.where(diff >= 0, jnp.exp(jnp.maximum(diff, 0.0) * lg), 0.0)
    pos = lax.broadcasted_iota(jnp.int32, (chunk, 1), 0).astype(F32)
    q_dec = jnp.exp((pos + 1.0) * lg)
    k_dec = jnp.exp((chunk - 1.0 - pos) * lg)
    chunk_dec = jnp.exp(chunk * lg)
    gn = gn_ref[...]

    for c in range(q_ref.shape[1] // chunk):
        rows = slice(c * chunk, (c + 1) * chunk)
        q = q_ref[0, rows, :]
        k = k_ref[0, rows, :]
        v = v_ref[0, rows, :]
        state = state_ref[...]
        s = _nt_dot(q, k) * decay
        o = _dot(s.astype(BF16), v) + _dot(q, state.astype(BF16)) * q_dec
        kd = (k.astype(F32) * k_dec).astype(BF16)
        state_ref[...] = state * chunk_dec + lax.dot_general(
            kd, v, (((0,), (0,)), ((), ())), preferred_element_type=F32)
        mu = jnp.mean(o, axis=-1, keepdims=True)
        oc = o - mu
        var = jnp.mean(oc * oc, axis=-1, keepdims=True)
        y = oc * lax.rsqrt(var + EPS) * gn * sg_ref[0, rows, :].astype(F32)
        o_ref[0, rows, :] = y.astype(BF16)


def _retention(qkvg, gn, lg_tab, d_model):
    b, t, width = qkvg.shape
    dk = d_model // RET_HEADS
    dv = 2 * dk
    rows = RET_ROWS
    qk_blocks = d_model // dk
    v_off = 2 * d_model // dv
    g_off = (2 * d_model + RET_HEADS * dv) // dv
    return pl.pallas_call(
        functools.partial(_retention_kernel, chunk=RET_CHUNK),
        grid=(b, RET_HEADS, t // rows),
        in_specs=[
            pl.BlockSpec((1, 8, LANE), lambda bi, hi, ci: (hi, 0, 0)),
            pl.BlockSpec((1, rows, dk), lambda bi, hi, ci: (bi, ci, hi)),
            pl.BlockSpec((1, rows, dk), lambda bi, hi, ci: (bi, ci, qk_blocks + hi)),
            pl.BlockSpec((1, rows, dv), lambda bi, hi, ci: (bi, ci, v_off + hi)),
            pl.BlockSpec((1, rows, dv), lambda bi, hi, ci: (bi, ci, g_off + hi)),
            pl.BlockSpec((1, dv), lambda bi, hi, ci: (0, hi)),
        ],
        out_specs=pl.BlockSpec((1, rows, dv), lambda bi, hi, ci: (bi, ci, hi)),
        out_shape=jax.ShapeDtypeStruct((b, t, RET_HEADS * dv), BF16),
        scratch_shapes=[pltpu.VMEM((dk, dv), F32)],
        compiler_params=_params(("parallel", "parallel", "arbitrary")),
        name="retention",
    )(lg_tab, qkvg, qkvg, qkvg, qkvg, gn.reshape(1, RET_HEADS * dv))


def _post_kernel(h_ref, a_ref, wo_ref, p_ref, gp_ref, wg_ref, we_ref, gf_ref, o_ref, *, final):
    h1 = h_ref[...] + _dot(a_ref[...], wo_ref[...])
    n = (_rms_unit(h1) * gp_ref[...]).astype(BF16)
    gate = jax.nn.sigmoid(_dot(n, wg_ref[...]))
    emb = _dot(p_ref[...].astype(BF16), we_ref[...])
    h2 = h1 + gate * emb
    if final:
        h2 = _rms_unit(h2) * gf_ref[...]
    o_ref[...] = h2


def _post(h, a, w_out, p, g_ple, w_gate, w_emb, g_final, final):
    n, d = h.shape
    wa = a.shape[1]
    pd = p.shape[1]
    tm = ROW_TILE
    return pl.pallas_call(
        functools.partial(_post_kernel, final=final),
        grid=(n // tm,),
        in_specs=[
            pl.BlockSpec((tm, d), lambda i: (i, 0)),
            pl.BlockSpec((tm, wa), lambda i: (i, 0)),
            _const_spec((wa, d)),
            pl.BlockSpec((tm, pd), lambda i: (i, 0)),
            _const_spec((1, d)),
            _const_spec((d, d)),
            _const_spec((pd, d)),
            _const_spec((1, d)),
        ],
        out_specs=pl.BlockSpec((tm, d), lambda i: (i, 0)),
        out_shape=jax.ShapeDtypeStruct((n, d), F32),
        compiler_params=_params(("parallel",)),
        name="post_final" if final else "post",
    )(h, a, w_out.astype(BF16), p, g_ple.reshape(1, d), w_gate.astype(BF16), w_emb.astype(BF16),
      g_final.reshape(1, d))


def _kv_proj_kernel(x_ref, g_ref, wk_ref, wvt_ref, ok_ref, ov_ref):
    xn = (_rms_unit(x_ref[0]) * g_ref[...]).astype(BF16)
    gw = NSA_GROUPS * NSA_DK
    for c in range(ok_ref.shape[0]):
        r = _dot(xn, wk_ref[:, c * gw:(c + 1) * gw])
        for g in range(NSA_GROUPS):
            ok_ref[c, 0, g] = r[:, g * NSA_DK:(g + 1) * NSA_DK].astype(BF16)
    for c in range(ov_ref.shape[0]):
        rt = _nt_dot(wvt_ref[c * gw:(c + 1) * gw, :], xn)
        for g in range(NSA_GROUPS):
            for tt in range(ov_ref.shape[3]):
                ov_ref[c, 0, g, tt] = rt[g * NSA_DK:(g + 1) * NSA_DK,
                                         tt * LANE:(tt + 1) * LANE].astype(BF16)


def _kv_proj(h3, g, w_rows, w_cols_t):
    b, t, d = h3.shape
    tm = ROW_TILE
    nr = w_rows.shape[1] // (NSA_GROUPS * NSA_DK)
    nc = w_cols_t.shape[0] // (NSA_GROUPS * NSA_DK)
    return pl.pallas_call(
        _kv_proj_kernel,
        grid=(b, t // tm),
        in_specs=[
            pl.BlockSpec((1, tm, d), lambda bi, i: (bi, i, 0)),
            _const_spec((1, d)),
            _const_spec(w_rows.shape),
            _const_spec(w_cols_t.shape),
        ],
        out_specs=[
            pl.BlockSpec((nr, 1, NSA_GROUPS, tm, NSA_DK), lambda bi, i: (0, bi, 0, i, 0)),
            pl.BlockSpec((nc, 1, NSA_GROUPS, tm // LANE, NSA_DK, LANE), lambda bi, i: (0, bi, 0, i, 0, 0)),
        ],
        out_shape=[
            jax.ShapeDtypeStruct((nr, b, NSA_GROUPS, t, NSA_DK), BF16),
            jax.ShapeDtypeStruct((nc, b, NSA_GROUPS, t // LANE, NSA_DK, LANE), BF16),
        ],
        compiler_params=_params(("parallel", "parallel")),
        name="kv_proj",
    )(h3, g.reshape(1, d), w_rows, w_cols_t)


def _compress_kernel(x_ref, pe_ref, w1_ref, w2_ref, w2t_ref, o_ref, ot_ref):
    sub = x_ref[0, 0, 0]
    half = sub.shape[1]
    w1 = w1_ref[0]
    first = _dot(sub, w1[:half, :])
    second = _dot(sub, w1[half:, :])
    rows = first.shape[0]
    second = pltpu.roll(second, rows - 1, 0)
    pe = jnp.broadcast_to(pe_ref[0], (8, pe_ref.shape[2])).astype(BF16)
    hid = first + second + _dot(pe, w1)[0:1, :]
    c0 = 0.7978845608028654
    act = 0.5 * hid * (1.0 + jnp.tanh(c0 * (hid + 0.044715 * hid * hid * hid)))
    act = act.astype(BF16)
    o_ref[0, 0, 0] = _dot(act, w2_ref[0]).astype(BF16)
    ot_ref[0, 0, 0] = _nt_dot(w2t_ref[0], act).astype(BF16)


def _compress(kv_rows, pe, w1, w2, w2t):
    _, b, g, t, dk = kv_rows.shape
    nb = t // CMP_STRIDE
    xr = kv_rows.reshape(kv_rows.shape[0], b, g, nb, CMP_STRIDE * dk)
    hid = w1.shape[2]
    return pl.pallas_call(
        _compress_kernel,
        grid=(2, b, g),
        in_specs=[
            pl.BlockSpec((1, 1, 1, nb, CMP_STRIDE * dk), lambda c, bi, gi: (c, bi, gi, 0, 0)),
            pl.BlockSpec((1, 1, CMP_LEN * dk), lambda c, bi, gi: (c, 0, 0)),
            pl.BlockSpec((1, CMP_LEN * dk, hid), lambda c, bi, gi: (c, 0, 0)),
            pl.BlockSpec((1, hid, dk), lambda c, bi, gi: (c, 0, 0)),
            pl.BlockSpec((1, dk, hid), lambda c, bi, gi: (c, 0, 0)),
        ],
        out_specs=[
            pl.BlockSpec((1, 1, 1, nb, dk), lambda c, bi, gi: (c, bi, gi, 0, 0)),
            pl.BlockSpec((1, 1, 1, dk, nb), lambda c, bi, gi: (c, bi, gi, 0, 0)),
        ],
        out_shape=[
            jax.ShapeDtypeStruct((2, b, g, nb, dk), BF16),
            jax.ShapeDtypeStruct((2, b, g, dk, nb), BF16),
        ],
        compiler_params=_params(("parallel", "parallel", "parallel")),
        name="compress",
    )(xr, pe, w1, w2, w2t)


def _nsa_inproj_kernel(x_ref, g_ref, wq_ref, wg_ref, wbt_ref, q_ref, sg_ref, bg_ref):
    xn = (_rms_unit(x_ref[0]) * g_ref[...]).astype(BF16)
    scale = float(NSA_DK) ** -0.5
    step = 512
    for c0 in range(0, wq_ref.shape[1], step):
        q_ref[0, :, c0:c0 + step] = (_dot(xn, wq_ref[:, c0:c0 + step]) * scale).astype(BF16)
    for c0 in range(0, wg_ref.shape[1], step):
        acc = _dot(xn, wg_ref[:, c0:c0 + step])
        sg_ref[0, :, c0:c0 + step] = (acc * jax.nn.sigmoid(acc)).astype(BF16)
    bg_ref[0] = jax.nn.sigmoid(_nt_dot(wbt_ref[...], xn))


def _nsa_inproj(h3, g, wq, wg, wbt):
    b, t, d = h3.shape
    tm = ROW_TILE
    return pl.pallas_call(
        _nsa_inproj_kernel,
        grid=(b, t // tm),
        in_specs=[
            pl.BlockSpec((1, tm, d), lambda bi, i: (bi, i, 0)),
            _const_spec((1, d)),
            _const_spec(wq.shape),
            _const_spec(wg.shape),
            _const_spec(wbt.shape),
        ],
        out_specs=[
            pl.BlockSpec((1, tm, wq.shape[1]), lambda bi, i: (bi, i, 0)),
            pl.BlockSpec((1, tm, wg.shape[1]), lambda bi, i: (bi, i, 0)),
            pl.BlockSpec((1, wbt.shape[0], tm), lambda bi, i: (bi, 0, i)),
        ],
        out_shape=[
            jax.ShapeDtypeStruct((b, t, wq.shape[1]), BF16),
            jax.ShapeDtypeStruct((b, t, wg.shape[1]), BF16),
            jax.ShapeDtypeStruct((b, wbt.shape[0], t), F32),
        ],
        compiler_params=_params(("parallel", "parallel")),
        name="nsa_inproj",
    )(h3, g.reshape(1, d), wq, wg, wbt)


def _softmax_cols(s_heads, bias, valid=None):
    probs = []
    for s in s_heads:
        sm = s + bias
        m = jnp.max(sm, axis=0, keepdims=True)
        e = jnp.exp(sm - m)
        if valid is not None:
            e = jnp.where(valid, e, 0.0)
        l = jnp.sum(e, axis=0, keepdims=True)
        probs.append(e * (1.0 / jnp.maximum(l, 1e-30)))
    return probs


def _nsa_core_kernel(q_ref, sg_ref, bg_ref, kc_ref, vct_ref, mt_ref, ks_ref, vst_ref, kw_ref, vwt_ref,
                     o_ref, bias_ref, acc_ref, *, n_top):
    blk = pl.program_id(2)
    heads = NSA_HPG
    qb = q_ref[0]
    q_all = jnp.concatenate([qb[:, r * NSA_DK:(r + 1) * NSA_DK] for r in range(heads)], axis=0)
    t_row = blk * Q_BLOCK + lax.broadcasted_iota(jnp.int32, (1, Q_BLOCK), 1)

    def head_slices(s):
        return [s[:, r * Q_BLOCK:(r + 1) * Q_BLOCK] for r in range(heads)]

    n_rows = kc_ref.shape[3]
    s_c = _nt_dot(kc_ref[0, 0, 0], q_all)
    n_idx = lax.broadcasted_iota(jnp.int32, (n_rows, Q_BLOCK), 0)
    valid_c = (n_idx * CMP_STRIDE + (CMP_LEN - 1)) <= t_row
    bias_c = jnp.where(valid_c, 0.0, NEG)
    p_c = _softmax_cols(head_slices(s_c), bias_c, valid_c)
    o_ct = _dot(vct_ref[0, 0, 0], jnp.concatenate([pc.astype(BF16) for pc in p_c], axis=1))
    p_sum = p_c[0]
    for pc in p_c[1:]:
        p_sum = p_sum + pc

    p_hi = p_sum.astype(BF16)
    p_lo = (p_sum - p_hi.astype(F32)).astype(BF16)
    mt = mt_ref[...]
    imp = _dot(mt, p_hi) + _dot(mt, p_lo)
    n_blocks = imp.shape[0]
    j_idx = lax.broadcasted_iota(jnp.int32, (n_blocks, Q_BLOCK), 0)
    j_f = j_idx.astype(F32)
    cur = jnp.right_shift(t_row, SEL_BLOCK.bit_length() - 1)
    forced = (j_idx == 0) | (j_idx == cur) | (j_idx == cur - 1)
    causal_blk = j_idx <= cur
    score = jnp.where(causal_blk, jnp.where(forced, BIG, imp), -BIG)
    picked = jnp.zeros_like(score)
    for _ in range(n_top):
        m = jnp.max(score, axis=0, keepdims=True)
        first = jnp.min(jnp.where(score == m, j_f, float(n_blocks)), axis=0, keepdims=True)
        hit = j_f == first
        picked = jnp.where(hit, 1.0, picked)
        score = jnp.where(hit, REMOVED, score)
    bias_ref[...] = jnp.where(causal_blk, jnp.where(picked > 0.5, 0.0, NEG), NEG)

    acc_ref[...] = jnp.zeros_like(acc_ref)
    blocks_per_tile = KEY_TILE // SEL_BLOCK
    lanes_per_tile = KEY_TILE // LANE
    key_iota = lax.broadcasted_iota(jnp.int32, (KEY_TILE, Q_BLOCK), 0)

    def tile_step(kt, carry):
        m_prev, l_prev = carry
        k0 = pl.multiple_of(kt * KEY_TILE, KEY_TILE)
        s = _nt_dot(ks_ref[0, 0, 0, pl.ds(k0, KEY_TILE), :], q_all)
        rows = bias_ref[pl.ds(pl.multiple_of(kt * blocks_per_tile, blocks_per_tile), blocks_per_tile), :]
        bias = jnp.concatenate(
            [jnp.broadcast_to(rows[jj:jj + 1, :], (SEL_BLOCK, Q_BLOCK)) for jj in range(blocks_per_tile)],
            axis=0)
        bias = jnp.where(k0 + key_iota <= t_row, bias, NEG)
        ps, ms, ls = [], [], []
        for r, s_r in enumerate(head_slices(s)):
            sm = s_r + bias
            mp = m_prev[:, r * Q_BLOCK:(r + 1) * Q_BLOCK]
            m_new = jnp.maximum(mp, jnp.max(sm, axis=0, keepdims=True))
            alpha = jnp.exp(mp - m_new)
            e = jnp.exp(sm - m_new)
            ls.append(alpha * l_prev[:, r * Q_BLOCK:(r + 1) * Q_BLOCK] + jnp.sum(e, axis=0, keepdims=True))
            ms.append(m_new)
            ps.append(e.astype(BF16))
            acc_ref[:, r * Q_BLOCK:(r + 1) * Q_BLOCK] = acc_ref[:, r * Q_BLOCK:(r + 1) * Q_BLOCK] * alpha
        vt = vst_ref[0, 0, 0, pl.ds(pl.multiple_of(kt * lanes_per_tile, lanes_per_tile), lanes_per_tile)]
        vt = jnp.concatenate([vt[i] for i in range(lanes_per_tile)], axis=1)
        acc_ref[...] += _dot(vt, jnp.concatenate(ps, axis=1))
        return jnp.concatenate(ms, axis=1), jnp.concatenate(ls, axis=1)

    n_tiles = (blk * Q_BLOCK + Q_BLOCK + KEY_TILE - 1) // KEY_TILE
    m0 = jnp.full((1, heads * Q_BLOCK), NEG, F32)
    l0 = jnp.zeros((1, heads * Q_BLOCK), F32)
    _, l_s = lax.fori_loop(0, n_tiles, tile_step, (m0, l0))

    span = WIN + Q_BLOCK
    w_blk = jnp.maximum(blk - WIN // Q_BLOCK, 0)
    w0 = pl.multiple_of(w_blk * Q_BLOCK, Q_BLOCK)
    s_w = _nt_dot(kw_ref[0, 0, 0, pl.ds(w0, span), :], q_all)
    kpos = w0 + lax.broadcasted_iota(jnp.int32, (span, Q_BLOCK), 0)
    bias_w = jnp.where((kpos <= t_row) & (kpos > t_row - WIN), 0.0, NEG)
    p_w = _softmax_cols(head_slices(s_w), bias_w)
    vw = vwt_ref[0, 0, 0, pl.ds(w_blk, span // LANE)]
    vw = jnp.concatenate([vw[i] for i in range(span // LANE)], axis=1)
    o_wt = _dot(vw, jnp.concatenate([pw.astype(BF16) for pw in p_w], axis=1))

    bg = bg_ref[0]
    sg = sg_ref[0]
    for r in range(heads):
        cols = slice(r * Q_BLOCK, (r + 1) * Q_BLOCK)
        g_c = bg[3 * r:3 * r + 1, :]
        g_s = bg[3 * r + 1:3 * r + 2, :]
        g_w = bg[3 * r + 2:3 * r + 3, :]
        o_t = (g_c * o_ct[:, cols] + (g_s * (1.0 / l_s[:, cols])) * acc_ref[:, cols] + g_w * o_wt[:, cols])
        o_ref[0, :, r * NSA_DK:(r + 1) * NSA_DK] = (
            o_t.T * sg[:, r * NSA_DK:(r + 1) * NSA_DK].astype(F32)).astype(BF16)


def _nsa_core(q, sg, bgt, cmp, cmp_t, mt, kv_rows, kv_cols_t, n_top):
    b, t, qw = q.shape
    gw = NSA_HPG * NSA_DK
    nb = cmp.shape[3]
    nsel = t // SEL_BLOCK
    nt = t // LANE
    return pl.pallas_call(
        functools.partial(_nsa_core_kernel, n_top=n_top),
        grid=(b, NSA_GROUPS, t // Q_BLOCK),
        in_specs=[
            pl.BlockSpec((1, Q_BLOCK, gw), lambda bi, gi, qi: (bi, qi, gi)),
            pl.BlockSpec((1, Q_BLOCK, gw), lambda bi, gi, qi: (bi, qi, gi)),
            pl.BlockSpec((1, GATE_ROWS, Q_BLOCK), lambda bi, gi, qi: (bi, gi, qi)),
            pl.BlockSpec((1, 1, 1, nb, NSA_DK), lambda bi, gi, qi: (0, bi, gi, 0, 0)),
            pl.BlockSpec((1, 1, 1, NSA_DK, nb), lambda bi, gi, qi: (1, bi, gi, 0, 0)),
            _const_spec(mt.shape),
            pl.BlockSpec((1, 1, 1, t, NSA_DK), lambda bi, gi, qi: (2, bi, gi, 0, 0)),
            pl.BlockSpec((1, 1, 1, nt, NSA_DK, LANE), lambda bi, gi, qi: (0, bi, gi, 0, 0, 0)),
            pl.BlockSpec((1, 1, 1, t, NSA_DK), lambda bi, gi, qi: (3, bi, gi, 0, 0)),
            pl.BlockSpec((1, 1, 1, nt, NSA_DK, LANE), lambda bi, gi, qi: (1, bi, gi, 0, 0, 0)),
        ],
        out_specs=pl.BlockSpec((1, Q_BLOCK, gw), lambda bi, gi, qi: (bi, qi, gi)),
        out_shape=jax.ShapeDtypeStruct((b, t, qw), BF16),
        scratch_shapes=[pltpu.VMEM((nsel, Q_BLOCK), F32), pltpu.VMEM((NSA_DK, NSA_HPG * Q_BLOCK), F32)],
        compiler_params=_params(("parallel", "parallel", "arbitrary")),
        name="nsa_core",
    )(q, sg, bgt, cmp, cmp_t, mt, kv_rows, kv_cols_t, kv_rows, kv_cols_t)


def _importance_matrix(n_sel_blocks, n_cmp_rows):
    f = SEL_BLOCK // CMP_STRIDE
    lc = CMP_LEN // CMP_STRIDE
    mt = np.zeros((n_sel_blocks, n_cmp_rows), np.float32)
    for j in range(n_sel_blocks):
        for o in range(-(lc - 1), f):
            n = f * j + o
            if 0 <= n < n_cmp_rows:
                mt[j, n] = float(min(o + lc, f) - max(o, 0))
    return jnp.asarray(mt, BF16)


def kernel(x, p, ret_norm, ret_w_in, ret_gn, ret_w_out, kv_norm, kv_w, cmp_pe_k, cmp_w1_k, cmp_w2_k,
           cmp_pe_v, cmp_w1_v, cmp_w2_v, nsa_norm, nsa_w_in, nsa_w_out, ple_norm, ple_w_gate, ple_w_emb,
           final_norm):
    b, t, d = x.shape
    depth = p.shape[0]
    n_ret = ret_norm.shape[0]
    n = b * t
    h = x.reshape(n, d)
    p2 = p.reshape(depth, n, p.shape[-1])

    half = d // RET_HEADS // 2
    inv = ROPE_BASE ** (-jnp.arange(half, dtype=F32) / half)
    ang = jnp.arange(t).astype(F32)[:, None] * inv[None, :]
    cos, sin = jnp.cos(ang), jnp.sin(ang)
    lg = jnp.log1p(-(2.0 ** (-5.0 - jnp.arange(RET_HEADS, dtype=F32))))
    lg_tab = jnp.broadcast_to(lg[:, None, None], (RET_HEADS, 8, LANE))

    shared = None
    for i in range(depth):
        last = i == depth - 1
        if i < n_ret:
            qkvg = _ret_inproj(h, ret_norm[i], ret_w_in[i], cos, sin, t)
            a = _retention(qkvg.reshape(b, t, -1), ret_gn[i], lg_tab, d)
            w_out = ret_w_out[i]
        else:
            j = i - n_ret
            h3 = h.reshape(b, t, d)
            if shared is None:
                gw = NSA_GROUPS * NSA_DK
                kv_wb = kv_w.astype(BF16)
                part = lambda c: kv_wb[:, c * gw:(c + 1) * gw]
                w_rows = jnp.concatenate([part(0), part(1), part(2), part(4)], axis=1)
                w_cols_t = jnp.concatenate([part(3), part(5)], axis=1).T
                kv_rows, kv_cols_t = _kv_proj(h3, kv_norm, w_rows, w_cols_t)
                pe = jnp.stack([cmp_pe_k.reshape(1, -1), cmp_pe_v.reshape(1, -1)])
                w1 = jnp.stack([cmp_w1_k, cmp_w1_v]).astype(BF16)
                w2 = jnp.stack([cmp_w2_k, cmp_w2_v]).astype(BF16)
                cmp, cmp_t = _compress(kv_rows, pe, w1, w2, jnp.swapaxes(w2, 1, 2))
                mt = _importance_matrix(t // SEL_BLOCK, t // CMP_STRIDE)
                shared = (cmp, cmp_t, mt, kv_rows, kv_cols_t)
            w_in = nsa_w_in[j].astype(BF16)
            qw = NSA_HEADS * NSA_DK
            wb = w_in[:, 2 * qw:].reshape(d, NSA_GROUPS, 3 * NSA_HPG)
            wb = jnp.pad(wb, ((0, 0), (0, 0), (0, GATE_ROWS - 3 * NSA_HPG)))
            wbt = wb.reshape(d, NSA_GROUPS * GATE_ROWS).T
            q, sg, bgt = _nsa_inproj(h3, nsa_norm[j], w_in[:, :qw], w_in[:, qw:2 * qw], wbt)
            a = _nsa_core(q, sg, bgt, *shared, n_top=min(N_SEL, t // SEL_BLOCK))
            w_out = nsa_w_out[j]
        h = _post(h, a.reshape(n, -1), w_out, p2[i], ple_norm[i], ple_w_gate[i], ple_w_emb[i],
                  final_norm, last)
    return h.reshape(b, t, d)
```

```python
import functools

import numpy as np
import jax
import jax.numpy as jnp
from jax import lax
from jax.experimental import pallas as pl
from jax.experimental.pallas import tpu as pltpu

F32 = jnp.float32
BF16 = jnp.bfloat16

EPS = 1e-6
NEG = -1e30
BIG = 1e30
REMOVED = -3e38

RET_HEADS = 4
RET_CHUNK = 128
ROPE_BASE = 10000.0

NSA_DK = 128
NSA_GROUPS = 4
NSA_HPG = 4
NSA_HEADS = NSA_GROUPS * NSA_HPG
CMP_LEN = 32
CMP_STRIDE = 16
SEL_BLOCK = 64
N_SEL = 16
WIN = 512
Q_BLOCK = 256
GATE_ROWS = 16
LOG2E = 1.4426950408889634
MASK_SLOTS = 128
SUM_ROWS = 16
CASE_TOKENS = 4096

LANE = 128
ROW_TILE = 512
KEY_TILE = 512
RET_ROWS = 512
VMEM_LIMIT = 56 * 1024 * 1024


def _params(sem):
    return pltpu.CompilerParams(dimension_semantics=sem, vmem_limit_bytes=VMEM_LIMIT)


def _const_spec(shape):
    nd = len(shape)
    return pl.BlockSpec(shape, lambda *_: (0,) * nd)


def _rms_unit(x):
    return x * lax.rsqrt(jnp.mean(x * x, axis=-1, keepdims=True) + EPS)


def _nt_dot(a, b):
    return lax.dot_general(a, b, (((1,), (1,)), ((), ())), preferred_element_type=F32)


def _dot(a, b):
    return jnp.dot(a, b, preferred_element_type=F32)


def _ret_inproj_kernel(x_ref, g_ref, w_ref, cos_ref, sin_ref, o_ref, *, qk_w, v_w):
    xn = (_rms_unit(x_ref[...]) * g_ref[...]).astype(BF16)
    cos = cos_ref[...]
    sin = sin_ref[...]
    half = cos.shape[-1]
    dk = 2 * half
    k_scale = float(dk) ** -0.5
    for hd in range(2 * qk_w // dk):
        c0 = hd * dk
        acc = _dot(xn, w_ref[:, c0:c0 + dk])
        x1 = acc[:, :half]
        x2 = acc[:, half:]
        r1 = x1 * cos - x2 * sin
        r2 = x1 * sin + x2 * cos
        if c0 >= qk_w:
            r1 = r1 * k_scale
            r2 = r2 * k_scale
        o_ref[:, c0:c0 + half] = r1.astype(BF16)
        o_ref[:, c0 + half:c0 + dk] = r2.astype(BF16)
    step = 512
    for c0 in range(2 * qk_w, 2 * qk_w + 2 * v_w, step):
        acc = _dot(xn, w_ref[:, c0:c0 + step])
        if c0 >= 2 * qk_w + v_w:
            acc = acc * jax.nn.sigmoid(acc)
        o_ref[:, c0:c0 + step] = acc.astype(BF16)


def _ret_inproj(h, g, w, cos, sin, seq):
    n, d = h.shape
    width = w.shape[1]
    qk_w = d
    v_w = (width - 2 * qk_w) // 2
    tm = ROW_TILE
    half = cos.shape[-1]
    pos_tiles = seq // tm
    return pl.pallas_call(
        functools.partial(_ret_inproj_kernel, qk_w=qk_w, v_w=v_w),
        grid=(n // tm,),
        in_specs=[
            pl.BlockSpec((tm, d), lambda i: (i, 0)),
            _const_spec((1, d)),
            _const_spec((d, width)),
            pl.BlockSpec((tm, half), lambda i: (i % pos_tiles, 0)),
            pl.BlockSpec((tm, half), lambda i: (i % pos_tiles, 0)),
        ],
        out_specs=pl.BlockSpec((tm, width), lambda i: (i, 0)),
        out_shape=jax.ShapeDtypeStruct((n, width), BF16),
        compiler_params=_params(("parallel",)),
        name="ret_inproj",
    )(h, g.reshape(1, d), w.astype(BF16), cos, sin)


def _retention_kernel(lg_ref, q_ref, k_ref, v_ref, sg_ref, gn_ref, o_ref, state_ref, *, chunk):
    @pl.when(pl.program_id(2) == 0)
    def _():
        state_ref[...] = jnp.zeros_like(state_ref)

    lg = lg_ref[0, 0:1, 0:1]
    row = lax.broadcasted_iota(jnp.int32, (chunk, chunk), 0)
    col = lax.broadcasted_iota(jnp.int32, (chunk, chunk), 1)
    diff = (row - col).astype(F32)
    decay = jnp.where(diff >= 0, jnp.exp(jnp.maximum(diff, 0.0) * lg), 0.0)
    pos = lax.broadcasted_iota(jnp.int32, (chunk, 1), 0).astype(F32)
    q_dec = jnp.exp((pos + 1.0) * lg)
    k_dec = jnp.exp((chunk - 1.0 - pos) * lg)
    chunk_dec = jnp.exp(chunk * lg)
    gn = gn_ref[...]

    for c in range(q_ref.shape[1] // chunk):
        rows = slice(c * chunk, (c + 1) * chunk)
        q = q_ref[0, rows, :]
        k = k_ref[0, rows, :]
        v = v_ref[0, rows, :]
        state = state_ref[...]
        s = _nt_dot(q, k) * decay
        o = _dot(s.astype(BF16), v) + _dot(q, state.astype(BF16)) * q_dec
        kd = (k.astype(F32) * k_dec).astype(BF16)
        state_ref[...] = state * chunk_dec + lax.dot_general(
            kd, v, (((0,), (0,)), ((), ())), preferred_element_type=F32)
        mu = jnp.mean(o, axis=-1, keepdims=True)
        oc = o - mu
        var = jnp.mean(oc * oc, axis=-1, keepdims=True)
        y = oc * lax.rsqrt(var + EPS) * gn * sg_ref[0, rows, :].astype(F32)
        o_ref[0, rows, :] = y.astype(BF16)


def _retention(qkvg, gn, lg_tab, d_model):
    b, t, width = qkvg.shape
    dk = d_model // RET_HEADS
    dv = 2 * dk
    rows = RET_ROWS
    qk_blocks = d_model // dk
    v_off = 2 * d_model // dv
    g_off = (2 * d_model + RET_HEADS * dv) // dv
    return pl.pallas_call(
        functools.partial(_retention_kernel, chunk=RET_CHUNK),
        grid=(b, RET_HEADS, t // rows),
        in_specs=[
            pl.BlockSpec((1, 8, LANE), lambda bi, hi, ci: (hi, 0, 0)),
            pl.BlockSpec((1, rows, dk), lambda bi, hi, ci: (bi, ci, hi)),
            pl.BlockSpec((1, rows, dk), lambda bi, hi, ci: (bi, ci, qk_blocks + hi)),
            pl.BlockSpec((1, rows, dv), lambda bi, hi, ci: (bi, ci, v_off + hi)),
            pl.BlockSpec((1, rows, dv), lambda bi, hi, ci: (bi, ci, g_off + hi)),
            pl.BlockSpec((1, dv), lambda bi, hi, ci: (0, hi)),
        ],
        out_specs=pl.BlockSpec((1, rows, dv), lambda bi, hi, ci: (bi, ci, hi)),
        out_shape=jax.ShapeDtypeStruct((b, t, RET_HEADS * dv), BF16),
        scratch_shapes=[pltpu.VMEM((dk, dv), F32)],
        compiler_params=_params(("parallel", "parallel", "arbitrary")),
        name="retention",
    )(lg_tab, qkvg, qkvg, qkvg, qkvg, gn.reshape(1, RET_HEADS * dv))


def _post_kernel(h_ref, a_ref, wo_ref, p_ref, gp_ref, wg_ref, we_ref, gf_ref, o_ref, *, final):
    h1 = h_ref[...] + _dot(a_ref[...], wo_ref[...])
    n = (_rms_unit(h1) * gp_ref[...]).astype(BF16)
    gate = jax.nn.sigmoid(_dot(n, wg_ref[...]))
    emb = _dot(p_ref[...].astype(BF16), we_ref[...])
    h2 = h1 + gate * emb
    if final:
        h2 = _rms_unit(h2) * gf_ref[...]
    o_ref[...] = h2


def _post(h, a, w_out, p, g_ple, w_gate, w_emb, g_final, final):
    n, d = h.shape
    wa = a.shape[1]
    pd = p.shape[1]
    tm = ROW_TILE
    return pl.pallas_call(
        functools.partial(_post_kernel, final=final),
        grid=(n // tm,),
        in_specs=[
            pl.BlockSpec((tm, d), lambda i: (i, 0)),
            pl.BlockSpec((tm, wa), lambda i: (i, 0)),
            _const_spec((wa, d)),
            pl.BlockSpec((tm, pd), lambda i: (i, 0)),
            _const_spec((1, d)),
            _const_spec((d, d)),
            _const_spec((pd, d)),
            _const_spec((1, d)),
        ],
        out_specs=pl.BlockSpec((tm, d), lambda i: (i, 0)),
        out_shape=jax.ShapeDtypeStruct((n, d), F32),
        compiler_params=_params(("parallel",)),
        name="post_final" if final else "post",
    )(h, a, w_out.astype(BF16), p, g_ple.reshape(1, d), w_gate.astype(BF16), w_emb.astype(BF16),
      g_final.reshape(1, d))


def _kv_proj_kernel(x_ref, g_ref, wk_ref, wvt_ref, ok_ref, ov_ref):
    xn = (_rms_unit(x_ref[0]) * g_ref[...]).astype(BF16)
    gw = NSA_GROUPS * NSA_DK
    for c in range(ok_ref.shape[0]):
        r = _dot(xn, wk_ref[:, c * gw:(c + 1) * gw])
        for g in range(NSA_GROUPS):
            ok_ref[c, 0, g] = r[:, g * NSA_DK:(g + 1) * NSA_DK].astype(BF16)
    for c in range(ov_ref.shape[0]):
        rt = _nt_dot(wvt_ref[c * gw:(c + 1) * gw, :], xn)
        for g in range(NSA_GROUPS):
            for tt in range(ov_ref.shape[3]):
                ov_ref[c, 0, g, tt] = rt[g * NSA_DK:(g + 1) * NSA_DK,
                                         tt * LANE:(tt + 1) * LANE].astype(BF16)


def _kv_proj(h3, g, w_rows, w_cols_t):
    b, t, d = h3.shape
    tm = ROW_TILE
    nr = w_rows.shape[1] // (NSA_GROUPS * NSA_DK)
    nc = w_cols_t.shape[0] // (NSA_GROUPS * NSA_DK)
    return pl.pallas_call(
        _kv_proj_kernel,
        grid=(b, t // tm),
        in_specs=[
            pl.BlockSpec((1, tm, d), lambda bi, i: (bi, i, 0)),
            _const_spec((1, d)),
            _const_spec(w_rows.shape),
            _const_spec(w_cols_t.shape),
        ],
        out_specs=[
            pl.BlockSpec((nr, 1, NSA_GROUPS, tm, NSA_DK), lambda bi, i: (0, bi, 0, i, 0)),
            pl.BlockSpec((nc, 1, NSA_GROUPS, tm // LANE, NSA_DK, LANE), lambda bi, i: (0, bi, 0, i, 0, 0)),
        ],
        out_shape=[
            jax.ShapeDtypeStruct((nr, b, NSA_GROUPS, t, NSA_DK), BF16),
            jax.ShapeDtypeStruct((nc, b, NSA_GROUPS, t // LANE, NSA_DK, LANE), BF16),
        ],
        compiler_params=_params(("parallel", "parallel")),
        name="kv_proj",
    )(h3, g.reshape(1, d), w_rows, w_cols_t)


def _compress_kernel(x_ref, pe_ref, w1_ref, w2_ref, w2t_ref, o_ref, ot_ref):
    sub = x_ref[0, 0, 0]
    half = sub.shape[1]
    w1 = w1_ref[0]
    first = _dot(sub, w1[:half, :])
    second = _dot(sub, w1[half:, :])
    rows = first.shape[0]
    second = pltpu.roll(second, rows - 1, 0)
    pe = jnp.broadcast_to(pe_ref[0], (8, pe_ref.shape[2])).astype(BF16)
    hid = first + second + _dot(pe, w1)[0:1, :]
    c0 = 0.7978845608028654
    act = 0.5 * hid * (1.0 + jnp.tanh(c0 * (hid + 0.044715 * hid * hid * hid)))
    act = act.astype(BF16)
    o_ref[0, 0, 0] = _dot(act, w2_ref[0]).astype(BF16)
    ot_ref[0, 0, 0] = _nt_dot(w2t_ref[0], act).astype(BF16)


def _compress(kv_rows, pe, w1, w2, w2t):
    _, b, g, t, dk = kv_rows.shape
    nb = t // CMP_STRIDE
    xr = kv_rows.reshape(kv_rows.shape[0], b, g, nb, CMP_STRIDE * dk)
    hid = w1.shape[2]
    return pl.pallas_call(
        _compress_kernel,
        grid=(2, b, g),
        in_specs=[
            pl.BlockSpec((1, 1, 1, nb, CMP_STRIDE * dk), lambda c, bi, gi: (c, bi, gi, 0, 0)),
            pl.BlockSpec((1, 1, CMP_LEN * dk), lambda c, bi, gi: (c, 0, 0)),
            pl.BlockSpec((1, CMP_LEN * dk, hid), lambda c, bi, gi: (c, 0, 0)),
            pl.BlockSpec((1, hid, dk), lambda c, bi, gi: (c, 0, 0)),
            pl.BlockSpec((1, dk, hid), lambda c, bi, gi: (c, 0, 0)),
        ],
        out_specs=[
            pl.BlockSpec((1, 1, 1, nb, dk), lambda c, bi, gi: (c, bi, gi, 0, 0)),
            pl.BlockSpec((1, 1, 1, dk, nb), lambda c, bi, gi: (c, bi, gi, 0, 0)),
        ],
        out_shape=[
            jax.ShapeDtypeStruct((2, b, g, nb, dk), BF16),
            jax.ShapeDtypeStruct((2, b, g, dk, nb), BF16),
        ],
        compiler_params=_params(("parallel", "parallel", "parallel")),
        name="compress",
    )(xr, pe, w1, w2, w2t)


def _nsa_inproj_kernel(x_ref, g_ref, wq_ref, wg_ref, wbt_ref, q_ref, sg_ref, bg_ref):
    xn = (_rms_unit(x_ref[0]) * g_ref[...]).astype(BF16)
    scale = float(NSA_DK) ** -0.5 * LOG2E
    step = 512
    for c0 in range(0, wq_ref.shape[1], step):
        q_ref[0, :, c0:c0 + step] = (_dot(xn, wq_ref[:, c0:c0 + step]) * scale).astype(BF16)
    for c0 in range(0, wg_ref.shape[1], step):
        acc = _dot(xn, wg_ref[:, c0:c0 + step])
        sg_ref[0, :, c0:c0 + step] = (acc * jax.nn.sigmoid(acc)).astype(BF16)
    bg_ref[0] = jax.nn.sigmoid(_nt_dot(wbt_ref[...], xn))


def _nsa_inproj(h3, g, wq, wg, wbt):
    b, t, d = h3.shape
    tm = ROW_TILE
    return pl.pallas_call(
        _nsa_inproj_kernel,
        grid=(b, t // tm),
        in_specs=[
            pl.BlockSpec((1, tm, d), lambda bi, i: (bi, i, 0)),
            _const_spec((1, d)),
            _const_spec(wq.shape),
            _const_spec(wg.shape),
            _const_spec(wbt.shape),
        ],
        out_specs=[
            pl.BlockSpec((1, tm, wq.shape[1]), lambda bi, i: (bi, i, 0)),
            pl.BlockSpec((1, tm, wg.shape[1]), lambda bi, i: (bi, i, 0)),
            pl.BlockSpec((1, wbt.shape[0], tm), lambda bi, i: (bi, 0, i)),
        ],
        out_shape=[
            jax.ShapeDtypeStruct((b, t, wq.shape[1]), BF16),
            jax.ShapeDtypeStruct((b, t, wg.shape[1]), BF16),
            jax.ShapeDtypeStruct((b, wbt.shape[0], t), F32),
        ],
        compiler_params=_params(("parallel", "parallel")),
        name="nsa_inproj",
    )(h3, g.reshape(1, d), wq, wg, wbt)


def _softmax_cols(s_heads, valid):
    probs = []
    for s in s_heads:
        sm = jnp.where(valid, s, NEG)
        m = jnp.max(sm, axis=0, keepdims=True)
        e = jnp.exp2(sm - m)
        l = jnp.sum(e, axis=0, keepdims=True)
        probs.append(e * jnp.where(m > 0.5 * NEG, 1.0 / l, 0.0))
    return probs


def _nsa_core_kernel(q_ref, sg_ref, bg_ref, kc_ref, vct_ref, mt_ref, ks_ref, et_ref, vst_ref, kw_ref, vwt_ref,
                     o_ref, qaug_ref, oc_ref, acc_ref, stat_ref, sa_ref, sb_ref, pa_ref, pb_ref,
                     *, n_top, n_cases):
    blk = pl.program_id(2)
    heads = NSA_HPG
    qb = q_ref[0]
    q_all = jnp.concatenate([qb[:, r * NSA_DK:(r + 1) * NSA_DK] for r in range(heads)], axis=0)
    t_row = blk * Q_BLOCK + lax.broadcasted_iota(jnp.int32, (1, Q_BLOCK), 1)

    def head_slices(s):
        return [s[:, r * Q_BLOCK:(r + 1) * Q_BLOCK] for r in range(heads)]

    def compress_and_select(rows_c, rows_s):
        s_c = _nt_dot(kc_ref[0, 0, 0, :rows_c, :], q_all)
        n_idx = lax.broadcasted_iota(jnp.int32, (rows_c, Q_BLOCK), 0)
        valid_c = (n_idx * CMP_STRIDE + (CMP_LEN - 1)) <= t_row
        p_c = _softmax_cols(head_slices(s_c), valid_c)
        oc_ref[...] = _dot(vct_ref[0, 0, 0, :, :rows_c],
                           jnp.concatenate([pc.astype(BF16) for pc in p_c], axis=1))
        p_sum = p_c[0]
        for pc in p_c[1:]:
            p_sum = p_sum + pc

        p_hi = p_sum.astype(BF16)
        p_lo = (p_sum - p_hi.astype(F32)).astype(BF16)
        mt = mt_ref[:rows_s, :rows_c]
        imp = _dot(mt, p_hi) + _dot(mt, p_lo)
        j_idx = lax.broadcasted_iota(jnp.int32, (rows_s, Q_BLOCK), 0)
        j_f = j_idx.astype(F32)
        cur = jnp.right_shift(t_row, SEL_BLOCK.bit_length() - 1)
        forced = (j_idx == 0) | (j_idx == cur) | (j_idx == cur - 1)
        causal_blk = j_idx <= cur
        score = jnp.where(causal_blk & jnp.logical_not(forced), imp, -BIG)
        for _ in range(n_top - 3):
            m = jnp.max(score, axis=0, keepdims=True)
            first = jnp.min(jnp.where(score == m, j_f, float(rows_s)), axis=0, keepdims=True)
            score = jnp.where(j_f == first, REMOVED, score)
        bias = jnp.where(causal_blk & (forced | (score == REMOVED)), 0.0, NEG)

        for h in range((rows_s + MASK_SLOTS - 1) // MASK_SLOTS):
            piece = bias[h * MASK_SLOTS:min((h + 1) * MASK_SLOTS, rows_s), :]
            if piece.shape[0] < MASK_SLOTS:
                piece = jnp.concatenate(
                    [piece, jnp.full((MASK_SLOTS - piece.shape[0], Q_BLOCK), NEG, F32)], axis=0)
            bq = piece.T.astype(BF16)
            qaug_ref[h] = jnp.concatenate([q_all, jnp.concatenate([bq] * heads, axis=0)], axis=1)

    rows_c_step = CASE_TOKENS // CMP_STRIDE
    rows_s_step = CASE_TOKENS // SEL_BLOCK
    for c in range(n_cases):
        pl.when(blk // (CASE_TOKENS // Q_BLOCK) == c)(functools.partial(
            compress_and_select,
            min(kc_ref.shape[3], rows_c_step * (c + 1)), min(mt_ref.shape[0], rows_s_step * (c + 1))))

    acc_ref[...] = jnp.zeros_like(acc_ref)
    lanes_per_tile = KEY_TILE // LANE
    tiles_per_half = MASK_SLOTS * SEL_BLOCK // KEY_TILE
    et_tiles = et_ref.shape[0] // KEY_TILE

    def scores(kt, s_dst):
        k0 = pl.multiple_of(kt * KEY_TILE, KEY_TILE)
        e0 = pl.multiple_of((kt % et_tiles) * KEY_TILE, KEY_TILE)
        k_aug = jnp.concatenate(
            [ks_ref[0, 0, 0, pl.ds(k0, KEY_TILE), :], et_ref[pl.ds(e0, KEY_TILE), :]], axis=1)
        s_dst[...] = _nt_dot(k_aug, qaug_ref[kt // tiles_per_half])

    def softmax_tile(kt, s_src, p_dst, m_prev, causal):
        if causal:
            key_pos = kt * KEY_TILE + lax.broadcasted_iota(jnp.int32, (KEY_TILE, Q_BLOCK), 0)
            causal_bias = jnp.where(key_pos <= t_row, 0.0, NEG)
        ms, alphas = [], []
        for r in range(heads):
            cols = slice(r * Q_BLOCK, (r + 1) * Q_BLOCK)
            sm = s_src[:, cols]
            if causal:
                sm = sm + causal_bias
            m_new = jnp.maximum(m_prev[:, cols], jnp.max(sm, axis=0, keepdims=True))
            ms.append(m_new)
            alphas.append(jnp.exp2(m_prev[:, cols] - m_new))
            p_dst[:, cols] = jnp.exp2((sm - m_new).astype(BF16))
        return jnp.concatenate(ms, axis=1), jnp.concatenate(alphas, axis=1)

    ones_rows = (lax.broadcasted_iota(jnp.int32, (SUM_ROWS, KEY_TILE), 0) == 0).astype(BF16)

    def accumulate(kt, p_src, alpha):
        vt = vst_ref[0, 0, 0, pl.ds(pl.multiple_of(kt * lanes_per_tile, lanes_per_tile), lanes_per_tile)]
        vt = jnp.concatenate([vt[i] for i in range(lanes_per_tile)], axis=1)
        vt = jnp.concatenate([vt, ones_rows], axis=0)
        acc_ref[...] = acc_ref[...] * alpha + _dot(vt, p_src[...])

    def step(kt, s_cur, p_cur, s_nxt, p_prv, carry):
        m_prev, alpha_prev = carry
        scores(kt + 1, s_nxt)
        accumulate(jnp.maximum(kt - 1, 0), p_prv, alpha_prev)
        return softmax_tile(kt, s_cur, p_cur, m_prev, False)

    def pair(i, carry):
        carry = step(2 * i, sa_ref, pa_ref, sb_ref, pb_ref, carry)
        return step(2 * i + 1, sb_ref, pb_ref, sa_ref, pa_ref, carry)

    def save(carry):
        for i, v in enumerate(carry):
            stat_ref[i:i + 1, :] = v

    def load():
        return tuple(stat_ref[i:i + 1, :] for i in range(2))

    n_tiles = (blk * Q_BLOCK + Q_BLOCK + KEY_TILE - 1) // KEY_TILE
    last = n_tiles - 1
    width = heads * Q_BLOCK
    pb_ref[...] = jnp.zeros_like(pb_ref)
    scores(0, sa_ref)
    carry = (jnp.full((1, width), NEG, F32), jnp.ones((1, width), F32))
    save(lax.fori_loop(0, last // 2, pair, carry))

    @pl.when(last % 2 == 1)
    def _():
        save(step(last - 1, sa_ref, pa_ref, sb_ref, pb_ref, load()))

    def finish(s_cur, p_cur, p_prv):
        m_run, alpha_prev = load()
        accumulate(jnp.maximum(last - 1, 0), p_prv, alpha_prev)
        _, alpha_last = softmax_tile(last, s_cur, p_cur, m_run, True)
        accumulate(last, p_cur, alpha_last)
        l_s = acc_ref[NSA_DK:NSA_DK + 1, :]

        span = WIN + Q_BLOCK
        w_blk = jnp.maximum(blk * (Q_BLOCK // LANE) - WIN // LANE, 0)
        w0 = pl.multiple_of(w_blk * LANE, LANE)
        s_w = _nt_dot(kw_ref[0, 0, 0, pl.ds(w0, span), :], q_all)
        kpos = w0 + lax.broadcasted_iota(jnp.int32, (span, Q_BLOCK), 0)
        p_w = _softmax_cols(head_slices(s_w), (kpos <= t_row) & (kpos > t_row - WIN))
        vw = vwt_ref[0, 0, 0, pl.ds(w_blk, span // LANE)]
        vw = jnp.concatenate([vw[i] for i in range(span // LANE)], axis=1)
        o_wt = _dot(vw, jnp.concatenate([pw.astype(BF16) for pw in p_w], axis=1))

        bg = bg_ref[0]
        sg = sg_ref[0]
        for r in range(heads):
            cols = slice(r * Q_BLOCK, (r + 1) * Q_BLOCK)
            g_c = bg[3 * r:3 * r + 1, :]
            g_s = bg[3 * r + 1:3 * r + 2, :]
            g_w = bg[3 * r + 2:3 * r + 3, :]
            o_t = (g_c * oc_ref[:, cols] + (g_s * (1.0 / l_s[:, cols])) * acc_ref[:NSA_DK, cols]
                   + g_w * o_wt[:, cols])
            o_ref[0, :, r * NSA_DK:(r + 1) * NSA_DK] = (
                o_t.T * sg[:, r * NSA_DK:(r + 1) * NSA_DK].astype(F32)).astype(BF16)

    pl.when(last % 2 == 0)(functools.partial(finish, sa_ref, pa_ref, pb_ref))
    pl.when(last % 2 == 1)(functools.partial(finish, sb_ref, pb_ref, pa_ref))


def _nsa_core(q, sg, bgt, cmp, cmp_t, mt, et, kv_rows, kv_cols_t, n_top):
    b, t, qw = q.shape
    gw = NSA_HPG * NSA_DK
    nb = cmp.shape[3]
    nsel = t // SEL_BLOCK
    nt = t // LANE
    halves = (nsel + MASK_SLOTS - 1) // MASK_SLOTS
    assert n_top >= 3 and t % Q_BLOCK == 0 and t >= WIN + Q_BLOCK, (n_top, t)
    n_cases = (t + CASE_TOKENS - 1) // CASE_TOKENS
    return pl.pallas_call(
        functools.partial(_nsa_core_kernel, n_top=n_top, n_cases=n_cases),
        grid=(b, NSA_GROUPS, t // Q_BLOCK),
        in_specs=[
            pl.BlockSpec((1, Q_BLOCK, gw), lambda bi, gi, qi: (bi, qi, gi)),
            pl.BlockSpec((1, Q_BLOCK, gw), lambda bi, gi, qi: (bi, qi, gi)),
            pl.BlockSpec((1, GATE_ROWS, Q_BLOCK), lambda bi, gi, qi: (bi, gi, qi)),
            pl.BlockSpec((1, 1, 1, nb, NSA_DK), lambda bi, gi, qi: (0, bi, gi, 0, 0)),
            pl.BlockSpec((1, 1, 1, NSA_DK, nb), lambda bi, gi, qi: (1, bi, gi, 0, 0)),
            _const_spec(mt.shape),
            pl.BlockSpec((1, 1, 1, t, NSA_DK), lambda bi, gi, qi: (2, bi, gi, 0, 0)),
            _const_spec(et.shape),
            pl.BlockSpec((1, 1, 1, nt, NSA_DK, LANE), lambda bi, gi, qi: (0, bi, gi, 0, 0, 0)),
            pl.BlockSpec((1, 1, 1, t, NSA_DK), lambda bi, gi, qi: (3, bi, gi, 0, 0)),
            pl.BlockSpec((1, 1, 1, nt, NSA_DK, LANE), lambda bi, gi, qi: (1, bi, gi, 0, 0, 0)),
        ],
        out_specs=pl.BlockSpec((1, Q_BLOCK, gw), lambda bi, gi, qi: (bi, qi, gi)),
        out_shape=jax.ShapeDtypeStruct((b, t, qw), BF16),
        scratch_shapes=[
            pltpu.VMEM((halves, NSA_HPG * Q_BLOCK, NSA_DK + MASK_SLOTS), BF16),
            pltpu.VMEM((NSA_DK, NSA_HPG * Q_BLOCK), F32),
            pltpu.VMEM((NSA_DK + SUM_ROWS, NSA_HPG * Q_BLOCK), F32),
            pltpu.VMEM((8, NSA_HPG * Q_BLOCK), F32),
            pltpu.VMEM((KEY_TILE, NSA_HPG * Q_BLOCK), F32),
            pltpu.VMEM((KEY_TILE, NSA_HPG * Q_BLOCK), F32),
            pltpu.VMEM((KEY_TILE, NSA_HPG * Q_BLOCK), BF16),
            pltpu.VMEM((KEY_TILE, NSA_HPG * Q_BLOCK), BF16),
        ],
        compiler_params=_params(("parallel", "parallel", "arbitrary")),
        name="nsa_core",
    )(q, sg, bgt, cmp, cmp_t, mt, kv_rows, et, kv_cols_t, kv_rows, kv_cols_t)


def _block_slot_table(seq):
    rows = min(seq, MASK_SLOTS * SEL_BLOCK)
    slot = (np.arange(rows) // SEL_BLOCK) % MASK_SLOTS
    return jnp.asarray(slot[:, None] == np.arange(MASK_SLOTS)[None, :], BF16)


def _importance_matrix(n_sel_blocks, n_cmp_rows):
    f = SEL_BLOCK // CMP_STRIDE
    lc = CMP_LEN // CMP_STRIDE
    mt = np.zeros((n_sel_blocks, n_cmp_rows), np.float32)
    for j in range(n_sel_blocks):
        for o in range(-(lc - 1), f):
            n = f * j + o
            if 0 <= n < n_cmp_rows:
                mt[j, n] = float(min(o + lc, f) - max(o, 0))
    return jnp.asarray(mt, BF16)


def kernel(x, p, ret_norm, ret_w_in, ret_gn, ret_w_out, kv_norm, kv_w, cmp_pe_k, cmp_w1_k, cmp_w2_k,
           cmp_pe_v, cmp_w1_v, cmp_w2_v, nsa_norm, nsa_w_in, nsa_w_out, ple_norm, ple_w_gate, ple_w_emb,
           final_norm):
    b, t, d = x.shape
    depth = p.shape[0]
    n_ret = ret_norm.shape[0]
    n = b * t
    h = x.reshape(n, d)
    p2 = p.reshape(depth, n, p.shape[-1])

    half = d // RET_HEADS // 2
    inv = ROPE_BASE ** (-jnp.arange(half, dtype=F32) / half)
    ang = jnp.arange(t).astype(F32)[:, None] * inv[None, :]
    cos, sin = jnp.cos(ang), jnp.sin(ang)
    lg = jnp.log1p(-(2.0 ** (-5.0 - jnp.arange(RET_HEADS, dtype=F32))))
    lg_tab = jnp.broadcast_to(lg[:, None, None], (RET_HEADS, 8, LANE))

    shared = None
    for i in range(depth):
        last = i == depth - 1
        if i < n_ret:
            qkvg = _ret_inproj(h, ret_norm[i], ret_w_in[i], cos, sin, t)
            a = _retention(qkvg.reshape(b, t, -1), ret_gn[i], lg_tab, d)
            w_out = ret_w_out[i]
        else:
            j = i - n_ret
            h3 = h.reshape(b, t, d)
            if shared is None:
                gw = NSA_GROUPS * NSA_DK
                kv_wb = kv_w.astype(BF16)
                part = lambda c: kv_wb[:, c * gw:(c + 1) * gw]
                w_rows = jnp.concatenate([part(0), part(1), part(2), part(4)], axis=1)
                w_cols_t = jnp.concatenate([part(3), part(5)], axis=1).T
                kv_rows, kv_cols_t = _kv_proj(h3, kv_norm, w_rows, w_cols_t)
                pe = jnp.stack([cmp_pe_k.reshape(1, -1), cmp_pe_v.reshape(1, -1)])
                w1 = jnp.stack([cmp_w1_k, cmp_w1_v]).astype(BF16)
                w2 = jnp.stack([cmp_w2_k, cmp_w2_v]).astype(BF16)
                cmp, cmp_t = _compress(kv_rows, pe, w1, w2, jnp.swapaxes(w2, 1, 2))
                mt = _importance_matrix(t // SEL_BLOCK, t // CMP_STRIDE)
                shared = (cmp, cmp_t, mt, _block_slot_table(t), kv_rows, kv_cols_t)
            w_in = nsa_w_in[j].astype(BF16)
            qw = NSA_HEADS * NSA_DK
            wb = w_in[:, 2 * qw:].reshape(d, NSA_GROUPS, 3 * NSA_HPG)
            wb = jnp.pad(wb, ((0, 0), (0, 0), (0, GATE_ROWS - 3 * NSA_HPG)))
            wbt = wb.reshape(d, NSA_GROUPS * GATE_ROWS).T
            q, sg, bgt = _nsa_inproj(h3, nsa_norm[j], w_in[:, :qw], w_in[:, qw:2 * qw], wbt)
            a = _nsa_core(q, sg, bgt, *shared, n_top=min(N_SEL, t // SEL_BLOCK))
            w_out = nsa_w_out[j]
        h = _post(h, a.reshape(n, -1), w_out, p2[i], ple_norm[i], ple_w_gate[i], ple_w_emb[i],
                  final_norm, last)
    return h.reshape(b, t, d)
```

```python
import functools

import numpy as np
import jax
import jax.numpy as jnp
from jax import lax
from jax.experimental import pallas as pl
from jax.experimental.pallas import tpu as pltpu

F32 = jnp.float32
BF16 = jnp.bfloat16

EPS = 1e-6
NEG = -1e30
BIG = 1e30
REMOVED = -3e38

RET_HEADS = 4
RET_CHUNK = 256
ROPE_BASE = 10000.0

NSA_DK = 128
NSA_GROUPS = 4
NSA_HPG = 4
NSA_HEADS = NSA_GROUPS * NSA_HPG
CMP_LEN = 32
CMP_STRIDE = 16
SEL_BLOCK = 64
N_SEL = 16
WIN = 512
Q_BLOCK = 512
GATE_ROWS = 16
LOG2E = 1.4426950408889634
MASK_SLOTS = 128
SUM_ROWS = 16
WIN_Q = 256
CASE_TOKENS = 4096

LANE = 128
ROW_TILE = 512
KEY_TILE = 512
RET_ROWS = 1024
VMEM_LIMIT = 56 * 1024 * 1024


def _params(sem):
    return pltpu.CompilerParams(dimension_semantics=sem, vmem_limit_bytes=VMEM_LIMIT)


def _const_spec(shape):
    nd = len(shape)
    return pl.BlockSpec(shape, lambda *_: (0,) * nd)


def _rms_unit(x):
    return x * lax.rsqrt(jnp.mean(x * x, axis=-1, keepdims=True) + EPS)


def _nt_dot(a, b):
    return lax.dot_general(a, b, (((1,), (1,)), ((), ())), preferred_element_type=F32)


def _dot(a, b):
    return jnp.dot(a, b, preferred_element_type=F32)


def _ret_inproj_kernel(x_ref, g_ref, w_ref, cos_ref, sin_ref, o_ref, *, qk_w, v_w):
    xn = (_rms_unit(x_ref[...]) * g_ref[...]).astype(BF16)
    cos = cos_ref[...]
    sin = sin_ref[...]
    half = cos.shape[-1]
    dk = 2 * half
    k_scale = float(dk) ** -0.5
    for hd in range(2 * qk_w // dk):
        c0 = hd * dk
        acc = _dot(xn, w_ref[:, c0:c0 + dk])
        x1 = acc[:, :half]
        x2 = acc[:, half:]
        r1 = x1 * cos - x2 * sin
        r2 = x1 * sin + x2 * cos
        if c0 >= qk_w:
            r1 = r1 * k_scale
            r2 = r2 * k_scale
        o_ref[:, c0:c0 + half] = r1.astype(BF16)
        o_ref[:, c0 + half:c0 + dk] = r2.astype(BF16)
    step = 512
    for c0 in range(2 * qk_w, 2 * qk_w + 2 * v_w, step):
        acc = _dot(xn, w_ref[:, c0:c0 + step])
        if c0 >= 2 * qk_w + v_w:
            acc = acc * jax.nn.sigmoid(acc)
        o_ref[:, c0:c0 + step] = acc.astype(BF16)


def _ret_inproj(h, g, w, cos, sin, seq):
    n, d = h.shape
    width = w.shape[1]
    qk_w = d
    v_w = (width - 2 * qk_w) // 2
    tm = ROW_TILE
    half = cos.shape[-1]
    pos_tiles = seq // tm
    return pl.pallas_call(
        functools.partial(_ret_inproj_kernel, qk_w=qk_w, v_w=v_w),
        grid=(n // tm,),
        in_specs=[
            pl.BlockSpec((tm, d), lambda i: (i, 0)),
            _const_spec((1, d)),
            _const_spec((d, width)),
            pl.BlockSpec((tm, half), lambda i: (i % pos_tiles, 0)),
            pl.BlockSpec((tm, half), lambda i: (i % pos_tiles, 0)),
        ],
        out_specs=pl.BlockSpec((tm, width), lambda i: (i, 0)),
        out_shape=jax.ShapeDtypeStruct((n, width), BF16),
        compiler_params=_params(("parallel",)),
        name="ret_inproj",
    )(h, g.reshape(1, d), w.astype(BF16), cos, sin)


def _retention_kernel(lg_ref, q_ref, k_ref, v_ref, sg_ref, gn_ref, o_ref, state_ref, *, chunk):
    @pl.when(pl.program_id(2) == 0)
    def _():
        state_ref[...] = jnp.zeros_like(state_ref)

    lg = lg_ref[0, 0:1, 0:1]
    row = lax.broadcasted_iota(jnp.int32, (chunk, chunk), 0)
    col = lax.broadcasted_iota(jnp.int32, (chunk, chunk), 1)
    diff = (row - col).astype(F32)
    decay = jnp.where(diff >= 0, jnp.exp(jnp.maximum(diff, 0.0) * lg), 0.0)
    pos = lax.broadcasted_iota(jnp.int32, (chunk, 1), 0).astype(F32)
    q_dec = jnp.exp((pos + 1.0) * lg)
    k_dec = jnp.exp((chunk - 1.0 - pos) * lg)
    chunk_dec = jnp.exp(chunk * lg)
    gn = gn_ref[...]

    for c in range(q_ref.shape[1] // chunk):
        rows = slice(c * chunk, (c + 1) * chunk)
        q = q_ref[0, rows, :]
        k = k_ref[0, rows, :]
        v = v_ref[0, rows, :]
        state = state_ref[...]
        s = _nt_dot(q, k) * decay
        o = _dot(s.astype(BF16), v) + _dot(q, state.astype(BF16)) * q_dec
        kd = (k.astype(F32) * k_dec).astype(BF16)
        state_ref[...] = state * chunk_dec + lax.dot_general(
            kd, v, (((0,), (0,)), ((), ())), preferred_element_type=F32)
        mu = jnp.mean(o, axis=-1, keepdims=True)
        oc = o - mu
        var = jnp.mean(oc * oc, axis=-1, keepdims=True)
        y = oc * lax.rsqrt(var + EPS) * gn * sg_ref[0, rows, :].astype(F32)
        o_ref[0, rows, :] = y.astype(BF16)


def _retention(qkvg, gn, lg_tab, d_model):
    b, t, width = qkvg.shape
    dk = d_model // RET_HEADS
    dv = 2 * dk
    rows = RET_ROWS
    qk_blocks = d_model // dk
    v_off = 2 * d_model // dv
    g_off = (2 * d_model + RET_HEADS * dv) // dv
    return pl.pallas_call(
        functools.partial(_retention_kernel, chunk=RET_CHUNK),
        grid=(b, RET_HEADS, t // rows),
        in_specs=[
            pl.BlockSpec((1, 8, LANE), lambda bi, hi, ci: (hi, 0, 0)),
            pl.BlockSpec((1, rows, dk), lambda bi, hi, ci: (bi, ci, hi)),
            pl.BlockSpec((1, rows, dk), lambda bi, hi, ci: (bi, ci, qk_blocks + hi)),
            pl.BlockSpec((1, rows, dv), lambda bi, hi, ci: (bi, ci, v_off + hi)),
            pl.BlockSpec((1, rows, dv), lambda bi, hi, ci: (bi, ci, g_off + hi)),
            pl.BlockSpec((1, dv), lambda bi, hi, ci: (0, hi)),
        ],
        out_specs=pl.BlockSpec((1, rows, dv), lambda bi, hi, ci: (bi, ci, hi)),
        out_shape=jax.ShapeDtypeStruct((b, t, RET_HEADS * dv), BF16),
        scratch_shapes=[pltpu.VMEM((dk, dv), F32)],
        compiler_params=_params(("parallel", "parallel", "arbitrary")),
        name="retention",
    )(lg_tab, qkvg, qkvg, qkvg, qkvg, gn.reshape(1, RET_HEADS * dv))


def _post_kernel(h_ref, a_ref, wo_ref, p_ref, gp_ref, wg_ref, we_ref, gf_ref, o_ref, *, final):
    h1 = h_ref[...] + _dot(a_ref[...], wo_ref[...])
    n = (_rms_unit(h1) * gp_ref[...]).astype(BF16)
    gate = jax.nn.sigmoid(_dot(n, wg_ref[...]))
    emb = _dot(p_ref[...].astype(BF16), we_ref[...])
    h2 = h1 + gate * emb
    if final:
        h2 = _rms_unit(h2) * gf_ref[...]
    o_ref[...] = h2


def _post(h, a, w_out, p, g_ple, w_gate, w_emb, g_final, final):
    n, d = h.shape
    wa = a.shape[1]
    pd = p.shape[1]
    tm = ROW_TILE
    return pl.pallas_call(
        functools.partial(_post_kernel, final=final),
        grid=(n // tm,),
        in_specs=[
            pl.BlockSpec((tm, d), lambda i: (i, 0)),
            pl.BlockSpec((tm, wa), lambda i: (i, 0)),
            _const_spec((wa, d)),
            pl.BlockSpec((tm, pd), lambda i: (i, 0)),
            _const_spec((1, d)),
            _const_spec((d, d)),
            _const_spec((pd, d)),
            _const_spec((1, d)),
        ],
        out_specs=pl.BlockSpec((tm, d), lambda i: (i, 0)),
        out_shape=jax.ShapeDtypeStruct((n, d), F32),
        compiler_params=_params(("parallel",)),
        name="post_final" if final else "post",
    )(h, a, w_out.astype(BF16), p, g_ple.reshape(1, d), w_gate.astype(BF16), w_emb.astype(BF16),
      g_final.reshape(1, d))


def _kv_proj_kernel(x_ref, g_ref, wk_ref, wvt_ref, oc_ref, ok_ref, ov_ref, r_ref):
    xn = (_rms_unit(x_ref[0]) * g_ref[...]).astype(BF16)
    gw = NSA_GROUPS * NSA_DK
    n_cmp = oc_ref.shape[0]
    blocks = oc_ref.shape[3]
    for c in range(n_cmp):
        r = _dot(xn, wk_ref[:, c * gw:(c + 1) * gw])
        for g in range(NSA_GROUPS):
            r_ref[g] = r[:, g * NSA_DK:(g + 1) * NSA_DK]
            for j in range(CMP_STRIDE):
                oc_ref[c, 0, g, :, j * NSA_DK:(j + 1) * NSA_DK] = r_ref[
                    g, pl.ds(j, blocks, stride=CMP_STRIDE), :].astype(BF16)
    for c in range(ok_ref.shape[0]):
        r = _dot(xn, wk_ref[:, (n_cmp + c) * gw:(n_cmp + c + 1) * gw])
        for g in range(NSA_GROUPS):
            ok_ref[c, 0, g] = r[:, g * NSA_DK:(g + 1) * NSA_DK].astype(BF16)
    for c in range(ov_ref.shape[0]):
        rt = _nt_dot(wvt_ref[c * gw:(c + 1) * gw, :], xn)
        for g in range(NSA_GROUPS):
            for tt in range(ov_ref.shape[3]):
                ov_ref[c, 0, g, tt] = rt[g * NSA_DK:(g + 1) * NSA_DK,
                                         tt * LANE:(tt + 1) * LANE].astype(BF16)


def _kv_proj(h3, g, w_rows, w_cols_t):
    b, t, d = h3.shape
    tm = ROW_TILE
    gw = NSA_GROUPS * NSA_DK
    nr = w_rows.shape[1] // gw - 2
    nc = w_cols_t.shape[0] // gw
    return pl.pallas_call(
        _kv_proj_kernel,
        grid=(b, t // tm),
        in_specs=[
            pl.BlockSpec((1, tm, d), lambda bi, i: (bi, i, 0)),
            _const_spec((1, d)),
            _const_spec(w_rows.shape),
            _const_spec(w_cols_t.shape),
        ],
        out_specs=[
            pl.BlockSpec((2, 1, NSA_GROUPS, tm // CMP_STRIDE, CMP_STRIDE * NSA_DK),
                         lambda bi, i: (0, bi, 0, i, 0)),
            pl.BlockSpec((nr, 1, NSA_GROUPS, tm, NSA_DK), lambda bi, i: (0, bi, 0, i, 0)),
            pl.BlockSpec((nc, 1, NSA_GROUPS, tm // LANE, NSA_DK, LANE), lambda bi, i: (0, bi, 0, i, 0, 0)),
        ],
        out_shape=[
            jax.ShapeDtypeStruct((2, b, NSA_GROUPS, t // CMP_STRIDE, CMP_STRIDE * NSA_DK), BF16),
            jax.ShapeDtypeStruct((nr, b, NSA_GROUPS, t, NSA_DK), BF16),
            jax.ShapeDtypeStruct((nc, b, NSA_GROUPS, t // LANE, NSA_DK, LANE), BF16),
        ],
        scratch_shapes=[pltpu.VMEM((NSA_GROUPS, tm, NSA_DK), F32)],
        compiler_params=_params(("parallel", "parallel")),
        name="kv_proj",
    )(h3, g.reshape(1, d), w_rows, w_cols_t)


def _compress_kernel(x_ref, pe_ref, w1_ref, w2_ref, w2t_ref, o_ref, ot_ref):
    sub = x_ref[0, 0, 0]
    half = sub.shape[1]
    w1 = w1_ref[0]
    first = _dot(sub, w1[:half, :])
    second = _dot(sub, w1[half:, :])
    rows = first.shape[0]
    second = pltpu.roll(second, rows - 1, 0)
    pe = jnp.broadcast_to(pe_ref[0], (8, pe_ref.shape[2])).astype(BF16)
    hid = first + second + _dot(pe, w1)[0:1, :]
    c0 = 0.7978845608028654
    act = 0.5 * hid * (1.0 + jnp.tanh(c0 * (hid + 0.044715 * hid * hid * hid)))
    act = act.astype(BF16)
    o_ref[0, 0, 0] = _dot(act, w2_ref[0]).astype(BF16)
    ot_ref[0, 0, 0] = _nt_dot(w2t_ref[0], act).astype(BF16)


def _compress(xr, pe, w1, w2, w2t):
    _, b, g, nb, row = xr.shape
    dk = row // CMP_STRIDE
    hid = w1.shape[2]
    return pl.pallas_call(
        _compress_kernel,
        grid=(2, b, g),
        in_specs=[
            pl.BlockSpec((1, 1, 1, nb, CMP_STRIDE * dk), lambda c, bi, gi: (c, bi, gi, 0, 0)),
            pl.BlockSpec((1, 1, CMP_LEN * dk), lambda c, bi, gi: (c, 0, 0)),
            pl.BlockSpec((1, CMP_LEN * dk, hid), lambda c, bi, gi: (c, 0, 0)),
            pl.BlockSpec((1, hid, dk), lambda c, bi, gi: (c, 0, 0)),
            pl.BlockSpec((1, dk, hid), lambda c, bi, gi: (c, 0, 0)),
        ],
        out_specs=[
            pl.BlockSpec((1, 1, 1, nb, dk), lambda c, bi, gi: (c, bi, gi, 0, 0)),
            pl.BlockSpec((1, 1, 1, dk, nb), lambda c, bi, gi: (c, bi, gi, 0, 0)),
        ],
        out_shape=[
            jax.ShapeDtypeStruct((2, b, g, nb, dk), BF16),
            jax.ShapeDtypeStruct((2, b, g, dk, nb), BF16),
        ],
        compiler_params=_params(("parallel", "parallel", "parallel")),
        name="compress",
    )(xr, pe, w1, w2, w2t)


def _nsa_inproj_kernel(x_ref, g_ref, wq_ref, wg_ref, wbt_ref, q_ref, sg_ref, bg_ref):
    xn = (_rms_unit(x_ref[0]) * g_ref[...]).astype(BF16)
    scale = float(NSA_DK) ** -0.5 * LOG2E
    step = 512
    for c0 in range(0, wq_ref.shape[1], step):
        q_ref[0, :, c0:c0 + step] = (_dot(xn, wq_ref[:, c0:c0 + step]) * scale).astype(BF16)
    for c0 in range(0, wg_ref.shape[1], step):
        acc = _dot(xn, wg_ref[:, c0:c0 + step])
        sg_ref[0, :, c0:c0 + step] = (acc * jax.nn.sigmoid(acc)).astype(BF16)
    bg_ref[0] = jax.nn.sigmoid(_nt_dot(wbt_ref[...], xn))


def _nsa_inproj(h3, g, wq, wg, wbt):
    b, t, d = h3.shape
    tm = ROW_TILE
    return pl.pallas_call(
        _nsa_inproj_kernel,
        grid=(b, t // tm),
        in_specs=[
            pl.BlockSpec((1, tm, d), lambda bi, i: (bi, i, 0)),
            _const_spec((1, d)),
            _const_spec(wq.shape),
            _const_spec(wg.shape),
            _const_spec(wbt.shape),
        ],
        out_specs=[
            pl.BlockSpec((1, tm, wq.shape[1]), lambda bi, i: (bi, i, 0)),
            pl.BlockSpec((1, tm, wg.shape[1]), lambda bi, i: (bi, i, 0)),
            pl.BlockSpec((1, wbt.shape[0], tm), lambda bi, i: (bi, 0, i)),
        ],
        out_shape=[
            jax.ShapeDtypeStruct((b, t, wq.shape[1]), BF16),
            jax.ShapeDtypeStruct((b, t, wg.shape[1]), BF16),
            jax.ShapeDtypeStruct((b, wbt.shape[0], t), F32),
        ],
        compiler_params=_params(("parallel", "parallel")),
        name="nsa_inproj",
    )(h3, g.reshape(1, d), wq, wg, wbt)


def _softmax_cols(s_heads, valid):
    probs = []
    for s in s_heads:
        sm = jnp.where(valid, s, NEG)
        m = jnp.max(sm, axis=0, keepdims=True)
        e = jnp.exp2(sm - m)
        l = jnp.sum(e, axis=0, keepdims=True)
        probs.append(e * jnp.where(m > 0.5 * NEG, 1.0 / l, 0.0))
    return probs


def _nsa_core_kernel(q_ref, sg_ref, bg_ref, kc_ref, vct_ref, mt_ref, ks_ref, et_ref, vst_ref, kw_ref, vwt_ref,
                     o_ref, qaug_ref, oc_ref, ow_ref, acc_ref, stat_ref, sa_ref, sb_ref, pa_ref, pb_ref,
                     *, n_top, n_cases):
    blk = pl.program_id(2)
    heads = NSA_HPG
    qb = q_ref[0]
    q_all = jnp.concatenate([qb[:, r * NSA_DK:(r + 1) * NSA_DK] for r in range(heads)], axis=0)
    t_row = blk * Q_BLOCK + lax.broadcasted_iota(jnp.int32, (1, Q_BLOCK), 1)

    def head_slices(s):
        return [s[:, r * Q_BLOCK:(r + 1) * Q_BLOCK] for r in range(heads)]

    def compress_and_select(rows_c, rows_s):
        s_c = _nt_dot(kc_ref[0, 0, 0, :rows_c, :], q_all)
        n_idx = lax.broadcasted_iota(jnp.int32, (rows_c, Q_BLOCK), 0)
        valid_c = (n_idx * CMP_STRIDE + (CMP_LEN - 1)) <= t_row
        p_c = _softmax_cols(head_slices(s_c), valid_c)
        oc_ref[...] = _dot(vct_ref[0, 0, 0, :, :rows_c],
                           jnp.concatenate([pc.astype(BF16) for pc in p_c], axis=1))
        p_sum = p_c[0]
        for pc in p_c[1:]:
            p_sum = p_sum + pc

        p_hi = p_sum.astype(BF16)
        p_lo = (p_sum - p_hi.astype(F32)).astype(BF16)
        mt = mt_ref[:rows_s, :rows_c]
        imp = _dot(mt, p_hi) + _dot(mt, p_lo)
        j_idx = lax.broadcasted_iota(jnp.int32, (rows_s, Q_BLOCK), 0)
        j_f = j_idx.astype(F32)
        cur = jnp.right_shift(t_row, SEL_BLOCK.bit_length() - 1)
        forced = (j_idx == 0) | (j_idx == cur) | (j_idx == cur - 1)
        causal_blk = j_idx <= cur
        score = jnp.where(causal_blk & jnp.logical_not(forced), imp, -BIG)
        for _ in range(n_top - 3):
            m = jnp.max(score, axis=0, keepdims=True)
            first = jnp.min(jnp.where(score == m, j_f, float(rows_s)), axis=0, keepdims=True)
            score = jnp.where(j_f == first, REMOVED, score)
        bias = jnp.where(causal_blk & (forced | (score == REMOVED)), 0.0, NEG)

        for h in range((rows_s + MASK_SLOTS - 1) // MASK_SLOTS):
            piece = bias[h * MASK_SLOTS:min((h + 1) * MASK_SLOTS, rows_s), :]
            if piece.shape[0] < MASK_SLOTS:
                piece = jnp.concatenate(
                    [piece, jnp.full((MASK_SLOTS - piece.shape[0], Q_BLOCK), NEG, F32)], axis=0)
            bq = piece.T.astype(BF16)
            qaug_ref[h] = jnp.concatenate([q_all, jnp.concatenate([bq] * heads, axis=0)], axis=1)

    rows_c_step = CASE_TOKENS // CMP_STRIDE
    rows_s_step = CASE_TOKENS // SEL_BLOCK
    for c in range(n_cases):
        pl.when(blk // (CASE_TOKENS // Q_BLOCK) == c)(functools.partial(
            compress_and_select,
            min(kc_ref.shape[3], rows_c_step * (c + 1)), min(mt_ref.shape[0], rows_s_step * (c + 1))))

    acc_ref[...] = jnp.zeros_like(acc_ref)
    lanes_per_tile = KEY_TILE // LANE
    tiles_per_half = MASK_SLOTS * SEL_BLOCK // KEY_TILE
    et_tiles = et_ref.shape[0] // KEY_TILE

    def scores(kt, s_dst):
        k0 = pl.multiple_of(kt * KEY_TILE, KEY_TILE)
        e0 = pl.multiple_of((kt % et_tiles) * KEY_TILE, KEY_TILE)
        k_aug = jnp.concatenate(
            [ks_ref[0, 0, 0, pl.ds(k0, KEY_TILE), :], et_ref[pl.ds(e0, KEY_TILE), :]], axis=1)
        s_dst[...] = _nt_dot(k_aug, qaug_ref[kt // tiles_per_half])

    def softmax_tile(kt, s_src, p_dst, m_prev, causal):
        if causal:
            key_pos = kt * KEY_TILE + lax.broadcasted_iota(jnp.int32, (KEY_TILE, Q_BLOCK), 0)
            causal_bias = jnp.where(key_pos <= t_row, 0.0, NEG)
        ms, alphas = [], []
        for r in range(heads):
            cols = slice(r * Q_BLOCK, (r + 1) * Q_BLOCK)
            sm = s_src[:, cols]
            if causal:
                sm = sm + causal_bias
            m_new = jnp.maximum(m_prev[:, cols], jnp.max(sm, axis=0, keepdims=True))
            ms.append(m_new)
            alphas.append(jnp.exp2(m_prev[:, cols] - m_new))
            p_dst[:, cols] = jnp.exp2((sm - m_new).astype(BF16))
        return jnp.concatenate(ms, axis=1), jnp.concatenate(alphas, axis=1)

    ones_rows = (lax.broadcasted_iota(jnp.int32, (SUM_ROWS, KEY_TILE), 0) == 0).astype(BF16)

    def accumulate(kt, p_src, alpha):
        vt = vst_ref[0, 0, 0, pl.ds(pl.multiple_of(kt * lanes_per_tile, lanes_per_tile), lanes_per_tile)]
        vt = jnp.concatenate([vt[i] for i in range(lanes_per_tile)], axis=1)
        vt = jnp.concatenate([vt, ones_rows], axis=0)
        acc_ref[...] = acc_ref[...] * alpha + _dot(vt, p_src[...])

    def step(kt, s_cur, p_cur, s_nxt, carry):
        m_prev, alpha_1, alpha_2 = carry
        scores(kt + 1, s_nxt)
        accumulate(jnp.maximum(kt - 2, 0), p_cur, alpha_2)
        m_new, alpha = softmax_tile(kt, s_cur, p_cur, m_prev, False)
        return m_new, alpha, alpha_1

    def pair(i, carry):
        carry = step(2 * i, sa_ref, pa_ref, sb_ref, carry)
        return step(2 * i + 1, sb_ref, pb_ref, sa_ref, carry)

    def save(carry):
        for i, v in enumerate(carry):
            stat_ref[i:i + 1, :] = v

    def load():
        return tuple(stat_ref[i:i + 1, :] for i in range(3))

    n_tiles = (blk * Q_BLOCK + Q_BLOCK + KEY_TILE - 1) // KEY_TILE
    last = n_tiles - 1
    width = heads * Q_BLOCK
    pa_ref[...] = jnp.zeros_like(pa_ref)
    pb_ref[...] = jnp.zeros_like(pb_ref)
    scores(0, sa_ref)

    span = WIN + WIN_Q
    ones_w = (lax.broadcasted_iota(jnp.int32, (SUM_ROWS, span), 0) == 0).astype(BF16)
    for h in range(Q_BLOCK // WIN_Q):
        q_sub = jnp.concatenate(
            [qb[h * WIN_Q:(h + 1) * WIN_Q, r * NSA_DK:(r + 1) * NSA_DK] for r in range(heads)], axis=0)
        t_sub = t_row[:, h * WIN_Q:(h + 1) * WIN_Q]
        w_blk = jnp.maximum((blk * Q_BLOCK + h * WIN_Q - WIN) // LANE, 0)
        w0 = pl.multiple_of(w_blk * LANE, LANE)
        s_w = _nt_dot(kw_ref[0, 0, 0, pl.ds(w0, span), :], q_sub)
        kpos = w0 + lax.broadcasted_iota(jnp.int32, (span, WIN_Q), 0)
        valid_w = (kpos <= t_sub) & (kpos > t_sub - WIN)
        p_w = []
        for r in range(heads):
            sm = jnp.where(valid_w, s_w[:, r * WIN_Q:(r + 1) * WIN_Q], NEG)
            m_w = jnp.max(sm, axis=0, keepdims=True)
            p_w.append(jnp.exp2((sm - m_w).astype(BF16)))
        vw = vwt_ref[0, 0, 0, pl.ds(w_blk, span // LANE)]
        vw = jnp.concatenate([vw[i] for i in range(span // LANE)] , axis=1)
        o_w = _dot(jnp.concatenate([vw, ones_w], axis=0), jnp.concatenate(p_w, axis=1))
        o_w = o_w[:NSA_DK] * (1.0 / o_w[NSA_DK:NSA_DK + 1])
        for r in range(heads):
            ow_ref[:, r * Q_BLOCK + h * WIN_Q:r * Q_BLOCK + (h + 1) * WIN_Q] = o_w[:, r * WIN_Q:(r + 1) * WIN_Q]

    ones = jnp.ones((1, width), F32)
    save(lax.fori_loop(0, last // 2, pair, (jnp.full((1, width), NEG, F32), ones, ones)))

    @pl.when(last % 2 == 1)
    def _():
        save(step(last - 1, sa_ref, pa_ref, sb_ref, load()))

    def finish(s_cur, p_cur, p_prv):
        m_run, alpha_1, alpha_2 = load()
        accumulate(jnp.maximum(last - 2, 0), p_cur, alpha_2)
        _, alpha_last = softmax_tile(last, s_cur, p_cur, m_run, True)
        accumulate(jnp.maximum(last - 1, 0), p_prv, alpha_1)
        accumulate(last, p_cur, alpha_last)

    pl.when(last % 2 == 0)(functools.partial(finish, sa_ref, pa_ref, pb_ref))
    pl.when(last % 2 == 1)(functools.partial(finish, sb_ref, pb_ref, pa_ref))

    l_s = acc_ref[NSA_DK:NSA_DK + 1, :]
    bg = bg_ref[0]
    sg = sg_ref[0]
    for r in range(heads):
        cols = slice(r * Q_BLOCK, (r + 1) * Q_BLOCK)
        g_c = bg[3 * r:3 * r + 1, :]
        g_s = bg[3 * r + 1:3 * r + 2, :]
        g_w = bg[3 * r + 2:3 * r + 3, :]
        o_t = (g_c * oc_ref[:, cols] + (g_s * (1.0 / l_s[:, cols])) * acc_ref[:NSA_DK, cols]
               + g_w * ow_ref[:, cols])
        o_ref[0, :, r * NSA_DK:(r + 1) * NSA_DK] = (
            o_t.T * sg[:, r * NSA_DK:(r + 1) * NSA_DK].astype(F32)).astype(BF16)


def _nsa_core(q, sg, bgt, cmp, cmp_t, mt, et, kv_rows, kv_cols_t, n_top):
    b, t, qw = q.shape
    gw = NSA_HPG * NSA_DK
    nb = cmp.shape[3]
    nsel = t // SEL_BLOCK
    nt = t // LANE
    halves = (nsel + MASK_SLOTS - 1) // MASK_SLOTS
    assert n_top >= 3 and t % Q_BLOCK == 0 and t >= WIN + Q_BLOCK, (n_top, t)
    single = dict(pipeline_mode=pl.Buffered(1))
    n_cases = (t + CASE_TOKENS - 1) // CASE_TOKENS
    return pl.pallas_call(
        functools.partial(_nsa_core_kernel, n_top=n_top, n_cases=n_cases),
        grid=(b, NSA_GROUPS, t // Q_BLOCK),
        in_specs=[
            pl.BlockSpec((1, Q_BLOCK, gw), lambda bi, gi, qi: (bi, qi, gi)),
            pl.BlockSpec((1, Q_BLOCK, gw), lambda bi, gi, qi: (bi, qi, gi)),
            pl.BlockSpec((1, GATE_ROWS, Q_BLOCK), lambda bi, gi, qi: (bi, gi, qi)),
            pl.BlockSpec((1, 1, 1, nb, NSA_DK), lambda bi, gi, qi: (0, bi, gi, 0, 0)),
            pl.BlockSpec((1, 1, 1, NSA_DK, nb), lambda bi, gi, qi: (1, bi, gi, 0, 0)),
            _const_spec(mt.shape),
            pl.BlockSpec((1, 1, 1, t, NSA_DK), lambda bi, gi, qi: (0, bi, gi, 0, 0), **single),
            pl.BlockSpec(et.shape, lambda *_: (0, 0), **single),
            pl.BlockSpec((1, 1, 1, nt, NSA_DK, LANE), lambda bi, gi, qi: (0, bi, gi, 0, 0, 0), **single),
            pl.BlockSpec((1, 1, 1, t, NSA_DK), lambda bi, gi, qi: (1, bi, gi, 0, 0), **single),
            pl.BlockSpec((1, 1, 1, nt, NSA_DK, LANE), lambda bi, gi, qi: (1, bi, gi, 0, 0, 0), **single),
        ],
        out_specs=pl.BlockSpec((1, Q_BLOCK, gw), lambda bi, gi, qi: (bi, qi, gi)),
        out_shape=jax.ShapeDtypeStruct((b, t, qw), BF16),
        scratch_shapes=[
            pltpu.VMEM((halves, NSA_HPG * Q_BLOCK, NSA_DK + MASK_SLOTS), BF16),
            pltpu.VMEM((NSA_DK, NSA_HPG * Q_BLOCK), F32),
            pltpu.VMEM((NSA_DK, NSA_HPG * Q_BLOCK), F32),
            pltpu.VMEM((NSA_DK + SUM_ROWS, NSA_HPG * Q_BLOCK), F32),
            pltpu.VMEM((8, NSA_HPG * Q_BLOCK), F32),
            pltpu.VMEM((KEY_TILE, NSA_HPG * Q_BLOCK), F32),
            pltpu.VMEM((KEY_TILE, NSA_HPG * Q_BLOCK), F32),
            pltpu.VMEM((KEY_TILE, NSA_HPG * Q_BLOCK), BF16),
            pltpu.VMEM((KEY_TILE, NSA_HPG * Q_BLOCK), BF16),
        ],
        compiler_params=_params(("parallel", "parallel", "arbitrary")),
        name="nsa_core",
    )(q, sg, bgt, cmp, cmp_t, mt, kv_rows, et, kv_cols_t, kv_rows, kv_cols_t)


def _block_slot_table(seq):
    rows = min(seq, MASK_SLOTS * SEL_BLOCK)
    slot = (np.arange(rows) // SEL_BLOCK) % MASK_SLOTS
    return jnp.asarray(slot[:, None] == np.arange(MASK_SLOTS)[None, :], BF16)


def _importance_matrix(n_sel_blocks, n_cmp_rows):
    f = SEL_BLOCK // CMP_STRIDE
    lc = CMP_LEN // CMP_STRIDE
    mt = np.zeros((n_sel_blocks, n_cmp_rows), np.float32)
    for j in range(n_sel_blocks):
        for o in range(-(lc - 1), f):
            n = f * j + o
            if 0 <= n < n_cmp_rows:
                mt[j, n] = float(min(o + lc, f) - max(o, 0))
    return jnp.asarray(mt, BF16)


def kernel(x, p, ret_norm, ret_w_in, ret_gn, ret_w_out, kv_norm, kv_w, cmp_pe_k, cmp_w1_k, cmp_w2_k,
           cmp_pe_v, cmp_w1_v, cmp_w2_v, nsa_norm, nsa_w_in, nsa_w_out, ple_norm, ple_w_gate, ple_w_emb,
           final_norm):
    b, t, d = x.shape
    depth = p.shape[0]
    n_ret = ret_norm.shape[0]
    n = b * t
    h = x.reshape(n, d)
    p2 = p.reshape(depth, n, p.shape[-1])

    half = d // RET_HEADS // 2
    inv = ROPE_BASE ** (-jnp.arange(half, dtype=F32) / half)
    ang = jnp.arange(t).astype(F32)[:, None] * inv[None, :]
    cos, sin = jnp.cos(ang), jnp.sin(ang)
    lg = jnp.log1p(-(2.0 ** (-5.0 - jnp.arange(RET_HEADS, dtype=F32))))
    lg_tab = jnp.broadcast_to(lg[:, None, None], (RET_HEADS, 8, LANE))

    shared = None
    for i in range(depth):
        last = i == depth - 1
        if i < n_ret:
            qkvg = _ret_inproj(h, ret_norm[i], ret_w_in[i], cos, sin, t)
            a = _retention(qkvg.reshape(b, t, -1), ret_gn[i], lg_tab, d)
            w_out = ret_w_out[i]
        else:
            j = i - n_ret
            h3 = h.reshape(b, t, d)
            if shared is None:
                gw = NSA_GROUPS * NSA_DK
                kv_wb = kv_w.astype(BF16)
                part = lambda c: kv_wb[:, c * gw:(c + 1) * gw]
                w_rows = jnp.concatenate([part(0), part(1), part(2), part(4)], axis=1)
                w_cols_t = jnp.concatenate([part(3), part(5)], axis=1).T
                cmp_in, kv_rows, kv_cols_t = _kv_proj(h3, kv_norm, w_rows, w_cols_t)
                pe = jnp.stack([cmp_pe_k.reshape(1, -1), cmp_pe_v.reshape(1, -1)])
                w1 = jnp.stack([cmp_w1_k, cmp_w1_v]).astype(BF16)
                w2 = jnp.stack([cmp_w2_k, cmp_w2_v]).astype(BF16)
                cmp, cmp_t = _compress(cmp_in, pe, w1, w2, jnp.swapaxes(w2, 1, 2))
                mt = _importance_matrix(t // SEL_BLOCK, t // CMP_STRIDE)
                shared = (cmp, cmp_t, mt, _block_slot_table(t), kv_rows, kv_cols_t)
            w_in = nsa_w_in[j].astype(BF16)
            qw = NSA_HEADS * NSA_DK
            wb = w_in[:, 2 * qw:].reshape(d, NSA_GROUPS, 3 * NSA_HPG)
            wb = jnp.pad(wb, ((0, 0), (0, 0), (0, GATE_ROWS - 3 * NSA_HPG)))
            wbt = wb.reshape(d, NSA_GROUPS * GATE_ROWS).T
            q, sg, bgt = _nsa_inproj(h3, nsa_norm[j], w_in[:, :qw], w_in[:, qw:2 * qw], wbt)
            a = _nsa_core(q, sg, bgt, *shared, n_top=min(N_SEL, t // SEL_BLOCK))
            w_out = nsa_w_out[j]
        h = _post(h, a.reshape(n, -1), w_out, p2[i], ple_norm[i], ple_w_gate[i], ple_w_emb[i],
                  final_norm, last)
    return h.reshape(b, t, d)
```

```python
import functools

import numpy as np
import jax
import jax.numpy as jnp
from jax import lax
from jax.experimental import pallas as pl
from jax.experimental.pallas import tpu as pltpu

F32 = jnp.float32
BF16 = jnp.bfloat16

EPS = 1e-6
NEG = -1e30
BIG = 1e30
REMOVED = -3e38

RET_HEADS = 4
RET_CHUNK = 256
ROPE_BASE = 10000.0

NSA_DK = 128
NSA_GROUPS = 4
NSA_HPG = 4
NSA_HEADS = NSA_GROUPS * NSA_HPG
CMP_LEN = 32
CMP_STRIDE = 16
SEL_BLOCK = 64
N_SEL = 16
WIN = 512
Q_BLOCK = 512
GATE_ROWS = 16
LOG2E = 1.4426950408889634
MASK_SLOTS = 128
SUM_ROWS = 16
WIN_Q = 256
CMP_CHUNK = 64
CASE_TOKENS = 4096

LANE = 128
ROW_TILE = 512
KEY_TILE = 512
RET_ROWS = 1024
VMEM_LIMIT = 56 * 1024 * 1024


def _params(sem):
    return pltpu.CompilerParams(dimension_semantics=sem, vmem_limit_bytes=VMEM_LIMIT)


def _const_spec(shape):
    nd = len(shape)
    return pl.BlockSpec(shape, lambda *_: (0,) * nd)


def _rms_unit(x):
    return x * lax.rsqrt(jnp.mean(x * x, axis=-1, keepdims=True) + EPS)


def _nt_dot(a, b):
    return lax.dot_general(a, b, (((1,), (1,)), ((), ())), preferred_element_type=F32)


def _dot(a, b):
    return jnp.dot(a, b, preferred_element_type=F32)


def _ret_inproj_kernel(x_ref, g_ref, w_ref, cos_ref, sin_ref, o_ref, *, qk_w, v_w):
    xn = (_rms_unit(x_ref[...]) * g_ref[...]).astype(BF16)
    cos = cos_ref[...]
    sin = sin_ref[...]
    half = cos.shape[-1]
    dk = 2 * half
    k_scale = float(dk) ** -0.5
    for hd in range(2 * qk_w // dk):
        c0 = hd * dk
        acc = _dot(xn, w_ref[:, c0:c0 + dk])
        x1 = acc[:, :half]
        x2 = acc[:, half:]
        r1 = x1 * cos - x2 * sin
        r2 = x1 * sin + x2 * cos
        if c0 >= qk_w:
            r1 = r1 * k_scale
            r2 = r2 * k_scale
        o_ref[:, c0:c0 + half] = r1.astype(BF16)
        o_ref[:, c0 + half:c0 + dk] = r2.astype(BF16)
    step = 512
    for c0 in range(2 * qk_w, 2 * qk_w + 2 * v_w, step):
        acc = _dot(xn, w_ref[:, c0:c0 + step])
        if c0 >= 2 * qk_w + v_w:
            acc = acc * jax.nn.sigmoid(acc)
        o_ref[:, c0:c0 + step] = acc.astype(BF16)


def _ret_inproj(h, g, w, cos, sin, seq):
    n, d = h.shape
    width = w.shape[1]
    qk_w = d
    v_w = (width - 2 * qk_w) // 2
    tm = ROW_TILE
    half = cos.shape[-1]
    pos_tiles = seq // tm
    return pl.pallas_call(
        functools.partial(_ret_inproj_kernel, qk_w=qk_w, v_w=v_w),
        grid=(n // tm,),
        in_specs=[
            pl.BlockSpec((tm, d), lambda i: (i, 0)),
            _const_spec((1, d)),
            _const_spec((d, width)),
            pl.BlockSpec((tm, half), lambda i: (i % pos_tiles, 0)),
            pl.BlockSpec((tm, half), lambda i: (i % pos_tiles, 0)),
        ],
        out_specs=pl.BlockSpec((tm, width), lambda i: (i, 0)),
        out_shape=jax.ShapeDtypeStruct((n, width), BF16),
        compiler_params=_params(("parallel",)),
        name="ret_inproj",
    )(h, g.reshape(1, d), w.astype(BF16), cos, sin)


def _retention_kernel(lg_ref, q_ref, k_ref, v_ref, sg_ref, gn_ref, o_ref, state_ref, *, chunk):
    @pl.when(pl.program_id(2) == 0)
    def _():
        state_ref[...] = jnp.zeros_like(state_ref)

    lg = lg_ref[0, 0:1, 0:1]
    row = lax.broadcasted_iota(jnp.int32, (chunk, chunk), 0)
    col = lax.broadcasted_iota(jnp.int32, (chunk, chunk), 1)
    diff = (row - col).astype(F32)
    decay = jnp.where(diff >= 0, jnp.exp(jnp.maximum(diff, 0.0) * lg), 0.0)
    pos = lax.broadcasted_iota(jnp.int32, (chunk, 1), 0).astype(F32)
    q_dec = jnp.exp((pos + 1.0) * lg)
    k_dec = jnp.exp((chunk - 1.0 - pos) * lg)
    chunk_dec = jnp.exp(chunk * lg)
    gn = gn_ref[...]

    for c in range(q_ref.shape[1] // chunk):
        rows = slice(c * chunk, (c + 1) * chunk)
        q = q_ref[0, rows, :]
        k = k_ref[0, rows, :]
        v = v_ref[0, rows, :]
        state = state_ref[...]
        s = _nt_dot(q, k) * decay
        o = _dot(s.astype(BF16), v) + _dot(q, state.astype(BF16)) * q_dec
        kd = (k.astype(F32) * k_dec).astype(BF16)
        state_ref[...] = state * chunk_dec + lax.dot_general(
            kd, v, (((0,), (0,)), ((), ())), preferred_element_type=F32)
        mu = jnp.mean(o, axis=-1, keepdims=True)
        oc = o - mu
        var = jnp.mean(oc * oc, axis=-1, keepdims=True)
        y = oc * lax.rsqrt(var + EPS) * gn * sg_ref[0, rows, :].astype(F32)
        o_ref[0, rows, :] = y.astype(BF16)


def _retention(qkvg, gn, lg_tab, d_model):
    b, t, width = qkvg.shape
    dk = d_model // RET_HEADS
    dv = 2 * dk
    rows = RET_ROWS
    qk_blocks = d_model // dk
    v_off = 2 * d_model // dv
    g_off = (2 * d_model + RET_HEADS * dv) // dv
    return pl.pallas_call(
        functools.partial(_retention_kernel, chunk=RET_CHUNK),
        grid=(b, RET_HEADS, t // rows),
        in_specs=[
            pl.BlockSpec((1, 8, LANE), lambda bi, hi, ci: (hi, 0, 0)),
            pl.BlockSpec((1, rows, dk), lambda bi, hi, ci: (bi, ci, hi)),
            pl.BlockSpec((1, rows, dk), lambda bi, hi, ci: (bi, ci, qk_blocks + hi)),
            pl.BlockSpec((1, rows, dv), lambda bi, hi, ci: (bi, ci, v_off + hi)),
            pl.BlockSpec((1, rows, dv), lambda bi, hi, ci: (bi, ci, g_off + hi)),
            pl.BlockSpec((1, dv), lambda bi, hi, ci: (0, hi)),
        ],
        out_specs=pl.BlockSpec((1, rows, dv), lambda bi, hi, ci: (bi, ci, hi)),
        out_shape=jax.ShapeDtypeStruct((b, t, RET_HEADS * dv), BF16),
        scratch_shapes=[pltpu.VMEM((dk, dv), F32)],
        compiler_params=_params(("parallel", "parallel", "arbitrary")),
        name="retention",
    )(lg_tab, qkvg, qkvg, qkvg, qkvg, gn.reshape(1, RET_HEADS * dv))


def _post_kernel(h_ref, a_ref, wo_ref, p_ref, gp_ref, wg_ref, we_ref, gf_ref, o_ref, *, final):
    h1 = h_ref[...] + _dot(a_ref[...], wo_ref[...])
    n = (_rms_unit(h1) * gp_ref[...]).astype(BF16)
    gate = jax.nn.sigmoid(_dot(n, wg_ref[...]))
    emb = _dot(p_ref[...].astype(BF16), we_ref[...])
    h2 = h1 + gate * emb
    if final:
        h2 = _rms_unit(h2) * gf_ref[...]
    o_ref[...] = h2


def _post(h, a, w_out, p, g_ple, w_gate, w_emb, g_final, final):
    n, d = h.shape
    wa = a.shape[1]
    pd = p.shape[1]
    tm = ROW_TILE
    return pl.pallas_call(
        functools.partial(_post_kernel, final=final),
        grid=(n // tm,),
        in_specs=[
            pl.BlockSpec((tm, d), lambda i: (i, 0)),
            pl.BlockSpec((tm, wa), lambda i: (i, 0)),
            _const_spec((wa, d)),
            pl.BlockSpec((tm, pd), lambda i: (i, 0)),
            _const_spec((1, d)),
            _const_spec((d, d)),
            _const_spec((pd, d)),
            _const_spec((1, d)),
        ],
        out_specs=pl.BlockSpec((tm, d), lambda i: (i, 0)),
        out_shape=jax.ShapeDtypeStruct((n, d), F32),
        compiler_params=_params(("parallel",)),
        name="post_final" if final else "post",
    )(h, a, w_out.astype(BF16), p, g_ple.reshape(1, d), w_gate.astype(BF16), w_emb.astype(BF16),
      g_final.reshape(1, d))


def _kv_proj_kernel(x_ref, g_ref, wk_ref, wvt_ref, oc_ref, ok_ref, ov_ref, r_ref):
    xn = (_rms_unit(x_ref[0]) * g_ref[...]).astype(BF16)
    gw = NSA_GROUPS * NSA_DK
    n_cmp = oc_ref.shape[0]
    blocks = oc_ref.shape[3]
    for c in range(n_cmp):
        r = _dot(xn, wk_ref[:, c * gw:(c + 1) * gw])
        for g in range(NSA_GROUPS):
            r_ref[g] = r[:, g * NSA_DK:(g + 1) * NSA_DK]
            for j in range(CMP_STRIDE):
                oc_ref[c, 0, g, :, j * NSA_DK:(j + 1) * NSA_DK] = r_ref[
                    g, pl.ds(j, blocks, stride=CMP_STRIDE), :].astype(BF16)
    for c in range(ok_ref.shape[0]):
        r = _dot(xn, wk_ref[:, (n_cmp + c) * gw:(n_cmp + c + 1) * gw])
        for g in range(NSA_GROUPS):
            ok_ref[c, 0, g] = r[:, g * NSA_DK:(g + 1) * NSA_DK].astype(BF16)
    for c in range(ov_ref.shape[0]):
        rt = _nt_dot(wvt_ref[c * gw:(c + 1) * gw, :], xn)
        for g in range(NSA_GROUPS):
            for tt in range(ov_ref.shape[3]):
                ov_ref[c, 0, g, tt] = rt[g * NSA_DK:(g + 1) * NSA_DK,
                                         tt * LANE:(tt + 1) * LANE].astype(BF16)


def _kv_proj(h3, g, w_rows, w_cols_t):
    b, t, d = h3.shape
    tm = ROW_TILE
    gw = NSA_GROUPS * NSA_DK
    nr = w_rows.shape[1] // gw - 2
    nc = w_cols_t.shape[0] // gw
    return pl.pallas_call(
        _kv_proj_kernel,
        grid=(b, t // tm),
        in_specs=[
            pl.BlockSpec((1, tm, d), lambda bi, i: (bi, i, 0)),
            _const_spec((1, d)),
            _const_spec(w_rows.shape),
            _const_spec(w_cols_t.shape),
        ],
        out_specs=[
            pl.BlockSpec((2, 1, NSA_GROUPS, tm // CMP_STRIDE, CMP_STRIDE * NSA_DK),
                         lambda bi, i: (0, bi, 0, i, 0)),
            pl.BlockSpec((nr, 1, NSA_GROUPS, tm, NSA_DK), lambda bi, i: (0, bi, 0, i, 0)),
            pl.BlockSpec((nc, 1, NSA_GROUPS, tm // LANE, NSA_DK, LANE), lambda bi, i: (0, bi, 0, i, 0, 0)),
        ],
        out_shape=[
            jax.ShapeDtypeStruct((2, b, NSA_GROUPS, t // CMP_STRIDE, CMP_STRIDE * NSA_DK), BF16),
            jax.ShapeDtypeStruct((nr, b, NSA_GROUPS, t, NSA_DK), BF16),
            jax.ShapeDtypeStruct((nc, b, NSA_GROUPS, t // LANE, NSA_DK, LANE), BF16),
        ],
        scratch_shapes=[pltpu.VMEM((NSA_GROUPS, tm, NSA_DK), F32)],
        compiler_params=_params(("parallel", "parallel")),
        name="kv_proj",
    )(h3, g.reshape(1, d), w_rows, w_cols_t)


def _compress_kernel(x_ref, pe_ref, w1_ref, w2_ref, w2t_ref, o_ref, ot_ref, sec_ref):
    sub = x_ref[0, 0, 0]
    half = sub.shape[1]
    w1 = w1_ref[0]
    rows = sub.shape[0]
    first = _dot(sub, w1[:half, :])
    sec_ref[0:rows, :] = _dot(sub, w1[half:, :])
    sec_ref[rows:rows + 8, :] = jnp.zeros((8, sec_ref.shape[1]), F32)
    second = sec_ref[1:rows + 1, :]
    pe = jnp.broadcast_to(pe_ref[0], (8, pe_ref.shape[2])).astype(BF16)
    hid = first + second + _dot(pe, w1)[0:1, :]
    c0 = 0.7978845608028654
    act = 0.5 * hid * (1.0 + jnp.tanh(c0 * (hid + 0.044715 * hid * hid * hid)))
    act = act.astype(BF16)
    o_ref[0, 0, 0] = _dot(act, w2_ref[0]).astype(BF16)
    ot_ref[0, 0, 0] = _nt_dot(w2t_ref[0], act).astype(BF16)


def _compress(xr, pe, w1, w2, w2t):
    _, b, g, nb, row = xr.shape
    dk = row // CMP_STRIDE
    hid = w1.shape[2]
    return pl.pallas_call(
        _compress_kernel,
        grid=(2, b, g),
        in_specs=[
            pl.BlockSpec((1, 1, 1, nb, CMP_STRIDE * dk), lambda c, bi, gi: (c, bi, gi, 0, 0)),
            pl.BlockSpec((1, 1, CMP_LEN * dk), lambda c, bi, gi: (c, 0, 0)),
            pl.BlockSpec((1, CMP_LEN * dk, hid), lambda c, bi, gi: (c, 0, 0)),
            pl.BlockSpec((1, hid, dk), lambda c, bi, gi: (c, 0, 0)),
            pl.BlockSpec((1, dk, hid), lambda c, bi, gi: (c, 0, 0)),
        ],
        out_specs=[
            pl.BlockSpec((1, 1, 1, nb, dk), lambda c, bi, gi: (c, bi, gi, 0, 0)),
            pl.BlockSpec((1, 1, 1, dk, nb), lambda c, bi, gi: (c, bi, gi, 0, 0)),
        ],
        out_shape=[
            jax.ShapeDtypeStruct((2, b, g, nb, dk), BF16),
            jax.ShapeDtypeStruct((2, b, g, dk, nb), BF16),
        ],
        scratch_shapes=[pltpu.VMEM((nb + 8, hid), F32)],
        compiler_params=_params(("parallel", "parallel", "parallel")),
        name="compress",
    )(xr, pe, w1, w2, w2t)


def _nsa_inproj_kernel(x_ref, g_ref, wq_ref, wg_ref, wbt_ref, q_ref, sg_ref, bg_ref):
    xn = (_rms_unit(x_ref[0]) * g_ref[...]).astype(BF16)
    scale = float(NSA_DK) ** -0.5 * LOG2E
    step = 512
    for c0 in range(0, wq_ref.shape[1], step):
        q_ref[0, :, c0:c0 + step] = (_dot(xn, wq_ref[:, c0:c0 + step]) * scale).astype(BF16)
    for c0 in range(0, wg_ref.shape[1], step):
        acc = _dot(xn, wg_ref[:, c0:c0 + step])
        sg_ref[0, :, c0:c0 + step] = (acc * jax.nn.sigmoid(acc)).astype(BF16)
    bg_ref[0] = jax.nn.sigmoid(_nt_dot(wbt_ref[...], xn))


def _nsa_inproj(h3, g, wq, wg, wbt):
    b, t, d = h3.shape
    tm = ROW_TILE
    return pl.pallas_call(
        _nsa_inproj_kernel,
        grid=(b, t // tm),
        in_specs=[
            pl.BlockSpec((1, tm, d), lambda bi, i: (bi, i, 0)),
            _const_spec((1, d)),
            _const_spec(wq.shape),
            _const_spec(wg.shape),
            _const_spec(wbt.shape),
        ],
        out_specs=[
            pl.BlockSpec((1, tm, wq.shape[1]), lambda bi, i: (bi, i, 0)),
            pl.BlockSpec((1, tm, wg.shape[1]), lambda bi, i: (bi, i, 0)),
            pl.BlockSpec((1, wbt.shape[0], tm), lambda bi, i: (bi, 0, i)),
        ],
        out_shape=[
            jax.ShapeDtypeStruct((b, t, wq.shape[1]), BF16),
            jax.ShapeDtypeStruct((b, t, wg.shape[1]), BF16),
            jax.ShapeDtypeStruct((b, wbt.shape[0], t), F32),
        ],
        compiler_params=_params(("parallel", "parallel")),
        name="nsa_inproj",
    )(h3, g.reshape(1, d), wq, wg, wbt)


def _nsa_core_kernel(q_ref, sg_ref, bg_ref, kc_ref, vct_ref, mt_ref, ks_ref, et_ref, vst_ref, kw_ref, vwt_ref,
                     o_ref, qaug_ref, oc_ref, ow_ref, acc_ref, stat_ref, sa_ref, sb_ref, pa_ref, pb_ref,
                     *, n_top, n_cases):
    blk = pl.program_id(2)
    heads = NSA_HPG
    qb = q_ref[0]
    q_all = jnp.concatenate([qb[:, r * NSA_DK:(r + 1) * NSA_DK] for r in range(heads)], axis=0)
    t_row = blk * Q_BLOCK + lax.broadcasted_iota(jnp.int32, (1, Q_BLOCK), 1)

    def head_slices(s):
        return [s[:, r * Q_BLOCK:(r + 1) * Q_BLOCK] for r in range(heads)]

    def compress_and_select(rows_c, rows_s):
        s_c = _nt_dot(kc_ref[0, 0, 0, :rows_c, :], q_all)
        chunks = [(c0, min(c0 + CMP_CHUNK, rows_c)) for c0 in range(0, rows_c, CMP_CHUNK)]
        row_iota = lax.broadcasted_iota(jnp.int32, (CMP_CHUNK, Q_BLOCK), 0)
        cmp_bias = [jnp.where(((row_iota[:c1 - c0] + c0) * CMP_STRIDE + (CMP_LEN - 1)) <= t_row, 0.0, NEG)
                    for c0, c1 in chunks]
        p_sum = [None] * len(chunks)
        p_heads = []
        for s in head_slices(s_c):
            sm = [s[c0:c1] + bc for bc, (c0, c1) in zip(cmp_bias, chunks)]
            top = sm[0]
            for x in sm[1:]:
                top = jnp.maximum(top, x)
            m = jnp.max(top, axis=0, keepdims=True)
            e = [jnp.exp2(x - m) for x in sm]
            tot = e[0]
            for x in e[1:]:
                tot = tot + x
            inv = jnp.where(m > 0.5 * NEG, 1.0 / jnp.sum(tot, axis=0, keepdims=True), 0.0)
            p = [x * inv for x in e]
            p_sum = [x if acc is None else acc + x for acc, x in zip(p_sum, p)]
            p_heads.append(jnp.concatenate([x.astype(BF16) for x in p], axis=0))
        oc_ref[...] = _dot(vct_ref[0, 0, 0, :, :rows_c], jnp.concatenate(p_heads, axis=1))

        p_hi = [x.astype(BF16) for x in p_sum]
        p_lo = jnp.concatenate([(x - h.astype(F32)).astype(BF16) for x, h in zip(p_sum, p_hi)], axis=0)
        p_hi = jnp.concatenate(p_hi, axis=0)
        mt = mt_ref[:rows_s, :rows_c]
        imp = _dot(mt, p_hi) + _dot(mt, p_lo)
        j_idx = lax.broadcasted_iota(jnp.int32, (rows_s, Q_BLOCK), 0)
        j_f = j_idx.astype(F32)
        cur = jnp.right_shift(t_row, SEL_BLOCK.bit_length() - 1)
        forced = (j_idx == 0) | (j_idx == cur) | (j_idx == cur - 1)
        causal_blk = j_idx <= cur
        score = jnp.where(causal_blk & jnp.logical_not(forced), imp, -BIG)
        for _ in range(n_top - 3):
            m = jnp.max(score, axis=0, keepdims=True)
            first = jnp.min(jnp.where(score == m, j_f, float(rows_s)), axis=0, keepdims=True)
            score = jnp.where(j_f == first, REMOVED, score)
        bias = jnp.where(causal_blk & (forced | (score == REMOVED)), 0.0, NEG)

        for h in range((rows_s + MASK_SLOTS - 1) // MASK_SLOTS):
            piece = bias[h * MASK_SLOTS:min((h + 1) * MASK_SLOTS, rows_s), :]
            if piece.shape[0] < MASK_SLOTS:
                piece = jnp.concatenate(
                    [piece, jnp.full((MASK_SLOTS - piece.shape[0], Q_BLOCK), NEG, F32)], axis=0)
            bq = piece.T.astype(BF16)
            qaug_ref[h] = jnp.concatenate([q_all, jnp.concatenate([bq] * heads, axis=0)], axis=1)

    rows_c_step = CASE_TOKENS // CMP_STRIDE
    rows_s_step = CASE_TOKENS // SEL_BLOCK
    for c in range(n_cases):
        pl.when(blk // (CASE_TOKENS // Q_BLOCK) == c)(functools.partial(
            compress_and_select,
            min(kc_ref.shape[3], rows_c_step * (c + 1)), min(mt_ref.shape[0], rows_s_step * (c + 1))))

    @pl.when((pl.program_id(0) == 0) & (pl.program_id(1) == 0) & (blk == 0))
    def _():
        acc_ref[...] = jnp.zeros_like(acc_ref)
        pa_ref[...] = jnp.zeros_like(pa_ref)
        pb_ref[...] = jnp.zeros_like(pb_ref)

    lanes_per_tile = KEY_TILE // LANE
    tiles_per_half = MASK_SLOTS * SEL_BLOCK // KEY_TILE
    et_tiles = et_ref.shape[0] // KEY_TILE

    def scores(kt, s_dst):
        k0 = pl.multiple_of(kt * KEY_TILE, KEY_TILE)
        e0 = pl.multiple_of((kt % et_tiles) * KEY_TILE, KEY_TILE)
        k_aug = jnp.concatenate(
            [ks_ref[0, 0, 0, pl.ds(k0, KEY_TILE), :], et_ref[pl.ds(e0, KEY_TILE), :]], axis=1)
        s_dst[...] = _nt_dot(k_aug, qaug_ref[kt // tiles_per_half])

    def softmax_tile(kt, s_src, p_dst, m_prev, causal):
        if causal:
            key_pos = kt * KEY_TILE + lax.broadcasted_iota(jnp.int32, (KEY_TILE, Q_BLOCK), 0)
            causal_bias = jnp.where(key_pos <= t_row, 0.0, NEG)
        ms, alphas = [], []
        for r in range(heads):
            cols = slice(r * Q_BLOCK, (r + 1) * Q_BLOCK)
            sm = s_src[:, cols]
            if causal:
                sm = sm + causal_bias
            m_new = jnp.maximum(m_prev[:, cols], jnp.max(sm, axis=0, keepdims=True))
            ms.append(m_new)
            alphas.append(jnp.exp2(m_prev[:, cols] - m_new))
            p_dst[:, cols] = jnp.exp2((sm - m_new).astype(BF16))
        return jnp.concatenate(ms, axis=1), jnp.concatenate(alphas, axis=1)

    ones_rows = (lax.broadcasted_iota(jnp.int32, (SUM_ROWS, KEY_TILE), 0) == 0).astype(BF16)

    def accumulate(kt, p_src, alpha):
        vt = vst_ref[0, 0, 0, pl.ds(pl.multiple_of(kt * lanes_per_tile, lanes_per_tile), lanes_per_tile)]
        vt = jnp.concatenate([vt[i] for i in range(lanes_per_tile)], axis=1)
        vt = jnp.concatenate([vt, ones_rows], axis=0)
        acc_ref[...] = acc_ref[...] * alpha + _dot(vt, p_src[...])

    def step(kt, s_cur, p_cur, s_nxt, carry, with_pv=True):
        m_prev, alpha_1, alpha_2 = carry
        scores(kt + 1, s_nxt)
        if with_pv:
            accumulate(jnp.maximum(kt - 2, 0), p_cur, alpha_2)
        m_new, alpha = softmax_tile(kt, s_cur, p_cur, m_prev, False)
        return m_new, alpha, alpha_1

    def pair(i, carry):
        carry = step(2 * i, sa_ref, pa_ref, sb_ref, carry)
        return step(2 * i + 1, sb_ref, pb_ref, sa_ref, carry)

    def save(carry):
        for i, v in enumerate(carry):
            stat_ref[i:i + 1, :] = v

    def load():
        return tuple(stat_ref[i:i + 1, :] for i in range(3))

    n_tiles = (blk * Q_BLOCK + Q_BLOCK + KEY_TILE - 1) // KEY_TILE
    last = n_tiles - 1
    width = heads * Q_BLOCK
    scores(0, sa_ref)

    span = WIN + WIN_Q
    ones_w = (lax.broadcasted_iota(jnp.int32, (SUM_ROWS, span), 0) == 0).astype(BF16)
    for h in range(Q_BLOCK // WIN_Q):
        q_sub = jnp.concatenate(
            [qb[h * WIN_Q:(h + 1) * WIN_Q, r * NSA_DK:(r + 1) * NSA_DK] for r in range(heads)], axis=0)
        t_sub = t_row[:, h * WIN_Q:(h + 1) * WIN_Q]
        w_blk = jnp.maximum((blk * Q_BLOCK + h * WIN_Q - WIN) // LANE, 0)
        w0 = pl.multiple_of(w_blk * LANE, LANE)
        s_w = _nt_dot(kw_ref[0, 0, 0, pl.ds(w0, span), :], q_sub)
        kpos = w0 + lax.broadcasted_iota(jnp.int32, (span, WIN_Q), 0)
        bias_w = jnp.where((kpos <= t_sub) & (kpos > t_sub - WIN), 0.0, NEG)
        p_w = []
        for r in range(heads):
            sm = s_w[:, r * WIN_Q:(r + 1) * WIN_Q] + bias_w
            m_w = jnp.max(sm, axis=0, keepdims=True)
            p_w.append(jnp.exp2((sm - m_w).astype(BF16)))
        vw = vwt_ref[0, 0, 0, pl.ds(w_blk, span // LANE)]
        vw = jnp.concatenate([vw[i] for i in range(span // LANE)], axis=1)
        o_w = _dot(jnp.concatenate([vw, ones_w], axis=0), jnp.concatenate(p_w, axis=1))
        o_w = o_w[:NSA_DK] * (1.0 / o_w[NSA_DK:NSA_DK + 1])
        for r in range(heads):
            ow_ref[:, r * Q_BLOCK + h * WIN_Q:r * Q_BLOCK + (h + 1) * WIN_Q] = o_w[:, r * WIN_Q:(r + 1) * WIN_Q]

    ones = jnp.ones((1, width), F32)
    start = (jnp.full((1, width), NEG, F32), ones, ones)

    @pl.when(last >= 2)
    def _():
        save(step(1, sb_ref, pb_ref, sa_ref, step(0, sa_ref, pa_ref, sb_ref, start, False), False))

    @pl.when(last < 2)
    def _():
        save(start)

    save(lax.fori_loop(1, last // 2, pair, load()))

    @pl.when(last % 2 == 1)
    def _():
        save(step(last - 1, sa_ref, pa_ref, sb_ref, load()))

    def finish(s_cur, p_cur, p_prv):
        m_run, alpha_1, alpha_2 = load()
        accumulate(jnp.maximum(last - 2, 0), p_cur, alpha_2)
        _, alpha_last = softmax_tile(last, s_cur, p_cur, m_run, True)
        accumulate(jnp.maximum(last - 1, 0), p_prv, alpha_1)
        accumulate(last, p_cur, alpha_last)

    pl.when(last % 2 == 0)(functools.partial(finish, sa_ref, pa_ref, pb_ref))
    pl.when(last % 2 == 1)(functools.partial(finish, sb_ref, pb_ref, pa_ref))

    l_s = acc_ref[NSA_DK:NSA_DK + 1, :]
    bg = bg_ref[0]
    sg = sg_ref[0]
    for r in range(heads):
        cols = slice(r * Q_BLOCK, (r + 1) * Q_BLOCK)
        g_c = bg[3 * r:3 * r + 1, :]
        g_s = bg[3 * r + 1:3 * r + 2, :]
        g_w = bg[3 * r + 2:3 * r + 3, :]
        o_t = (g_c * oc_ref[:, cols] + (g_s * (1.0 / l_s[:, cols])) * acc_ref[:NSA_DK, cols]
               + g_w * ow_ref[:, cols])
        o_ref[0, :, r * NSA_DK:(r + 1) * NSA_DK] = (
            o_t.T * sg[:, r * NSA_DK:(r + 1) * NSA_DK].astype(F32)).astype(BF16)


def _nsa_core(q, sg, bgt, cmp, cmp_t, mt, et, kv_rows, kv_cols_t, n_top):
    b, t, qw = q.shape
    gw = NSA_HPG * NSA_DK
    nb = cmp.shape[3]
    nsel = t // SEL_BLOCK
    nt = t // LANE
    halves = (nsel + MASK_SLOTS - 1) // MASK_SLOTS
    assert n_top >= 3 and t % Q_BLOCK == 0 and t >= WIN + Q_BLOCK, (n_top, t)
    single = dict(pipeline_mode=pl.Buffered(1))
    n_cases = (t + CASE_TOKENS - 1) // CASE_TOKENS
    return pl.pallas_call(
        functools.partial(_nsa_core_kernel, n_top=n_top, n_cases=n_cases),
        grid=(b, NSA_GROUPS, t // Q_BLOCK),
        in_specs=[
            pl.BlockSpec((1, Q_BLOCK, gw), lambda bi, gi, qi: (bi, qi, gi)),
            pl.BlockSpec((1, Q_BLOCK, gw), lambda bi, gi, qi: (bi, qi, gi)),
            pl.BlockSpec((1, GATE_ROWS, Q_BLOCK), lambda bi, gi, qi: (bi, gi, qi)),
            pl.BlockSpec((1, 1, 1, nb, NSA_DK), lambda bi, gi, qi: (0, bi, gi, 0, 0)),
            pl.BlockSpec((1, 1, 1, NSA_DK, nb), lambda bi, gi, qi: (1, bi, gi, 0, 0)),
            _const_spec(mt.shape),
            pl.BlockSpec((1, 1, 1, t, NSA_DK), lambda bi, gi, qi: (0, bi, gi, 0, 0), **single),
            pl.BlockSpec(et.shape, lambda *_: (0, 0), **single),
            pl.BlockSpec((1, 1, 1, nt, NSA_DK, LANE), lambda bi, gi, qi: (0, bi, gi, 0, 0, 0), **single),
            pl.BlockSpec((1, 1, 1, t, NSA_DK), lambda bi, gi, qi: (1, bi, gi, 0, 0), **single),
            pl.BlockSpec((1, 1, 1, nt, NSA_DK, LANE), lambda bi, gi, qi: (1, bi, gi, 0, 0, 0), **single),
        ],
        out_specs=pl.BlockSpec((1, Q_BLOCK, gw), lambda bi, gi, qi: (bi, qi, gi)),
        out_shape=jax.ShapeDtypeStruct((b, t, qw), BF16),
        scratch_shapes=[
            pltpu.VMEM((halves, NSA_HPG * Q_BLOCK, NSA_DK + MASK_SLOTS), BF16),
            pltpu.VMEM((NSA_DK, NSA_HPG * Q_BLOCK), F32),
            pltpu.VMEM((NSA_DK, NSA_HPG * Q_BLOCK), F32),
            pltpu.VMEM((NSA_DK + SUM_ROWS, NSA_HPG * Q_BLOCK), F32),
            pltpu.VMEM((8, NSA_HPG * Q_BLOCK), F32),
            pltpu.VMEM((KEY_TILE, NSA_HPG * Q_BLOCK), F32),
            pltpu.VMEM((KEY_TILE, NSA_HPG * Q_BLOCK), F32),
            pltpu.VMEM((KEY_TILE, NSA_HPG * Q_BLOCK), BF16),
            pltpu.VMEM((KEY_TILE, NSA_HPG * Q_BLOCK), BF16),
        ],
        compiler_params=_params(("arbitrary", "arbitrary", "arbitrary")),
        name="nsa_core",
    )(q, sg, bgt, cmp, cmp_t, mt, kv_rows, et, kv_cols_t, kv_rows, kv_cols_t)


def _block_slot_table(seq):
    rows = min(seq, MASK_SLOTS * SEL_BLOCK)
    slot = (np.arange(rows) // SEL_BLOCK) % MASK_SLOTS
    return jnp.asarray(slot[:, None] == np.arange(MASK_SLOTS)[None, :], BF16)


def _importance_matrix(n_sel_blocks, n_cmp_rows):
    f = SEL_BLOCK // CMP_STRIDE
    lc = CMP_LEN // CMP_STRIDE
    mt = np.zeros((n_sel_blocks, n_cmp_rows), np.float32)
    for j in range(n_sel_blocks):
        for o in range(-(lc - 1), f):
            n = f * j + o
            if 0 <= n < n_cmp_rows:
                mt[j, n] = float(min(o + lc, f) - max(o, 0))
    return jnp.asarray(mt, BF16)


def kernel(x, p, ret_norm, ret_w_in, ret_gn, ret_w_out, kv_norm, kv_w, cmp_pe_k, cmp_w1_k, cmp_w2_k,
           cmp_pe_v, cmp_w1_v, cmp_w2_v, nsa_norm, nsa_w_in, nsa_w_out, ple_norm, ple_w_gate, ple_w_emb,
           final_norm):
    b, t, d = x.shape
    depth = p.shape[0]
    n_ret = ret_norm.shape[0]
    n = b * t
    h = x.reshape(n, d)
    p2 = p.reshape(depth, n, p.shape[-1])

    half = d // RET_HEADS // 2
    inv = ROPE_BASE ** (-jnp.arange(half, dtype=F32) / half)
    ang = jnp.arange(t).astype(F32)[:, None] * inv[None, :]
    cos, sin = jnp.cos(ang), jnp.sin(ang)
    lg = jnp.log1p(-(2.0 ** (-5.0 - jnp.arange(RET_HEADS, dtype=F32))))
    lg_tab = jnp.broadcast_to(lg[:, None, None], (RET_HEADS, 8, LANE))

    shared = None
    for i in range(depth):
        last = i == depth - 1
        if i < n_ret:
            qkvg = _ret_inproj(h, ret_norm[i], ret_w_in[i], cos, sin, t)
            a = _retention(qkvg.reshape(b, t, -1), ret_gn[i], lg_tab, d)
            w_out = ret_w_out[i]
        else:
            j = i - n_ret
            h3 = h.reshape(b, t, d)
            if shared is None:
                gw = NSA_GROUPS * NSA_DK
                kv_wb = kv_w.astype(BF16)
                part = lambda c: kv_wb[:, c * gw:(c + 1) * gw]
                w_rows = jnp.concatenate([part(0), part(1), part(2), part(4)], axis=1)
                w_cols_t = jnp.concatenate([part(3), part(5)], axis=1).T
                cmp_in, kv_rows, kv_cols_t = _kv_proj(h3, kv_norm, w_rows, w_cols_t)
                pe = jnp.stack([cmp_pe_k.reshape(1, -1), cmp_pe_v.reshape(1, -1)])
                w1 = jnp.stack([cmp_w1_k, cmp_w1_v]).astype(BF16)
                w2 = jnp.stack([cmp_w2_k, cmp_w2_v]).astype(BF16)
                cmp, cmp_t = _compress(cmp_in, pe, w1, w2, jnp.swapaxes(w2, 1, 2))
                mt = _importance_matrix(t // SEL_BLOCK, t // CMP_STRIDE)
                shared = (cmp, cmp_t, mt, _block_slot_table(t), kv_rows, kv_cols_t)
            w_in = nsa_w_in[j].astype(BF16)
            qw = NSA_HEADS * NSA_DK
            wb = w_in[:, 2 * qw:].reshape(d, NSA_GROUPS, 3 * NSA_HPG)
            wb = jnp.pad(wb, ((0, 0), (0, 0), (0, GATE_ROWS - 3 * NSA_HPG)))
            wbt = wb.reshape(d, NSA_GROUPS * GATE_ROWS).T
            q, sg, bgt = _nsa_inproj(h3, nsa_norm[j], w_in[:, :qw], w_in[:, qw:2 * qw], wbt)
            a = _nsa_core(q, sg, bgt, *shared, n_top=min(N_SEL, t // SEL_BLOCK))
            w_out = nsa_w_out[j]
        h = _post(h, a.reshape(n, -1), w_out, p2[i], ple_norm[i], ple_w_gate[i], ple_w_emb[i],
                  final_norm, last)
    return h.reshape(b, t, d)
```
